```python
import math
import functools
import jax
import jax.numpy as jnp
from jax import lax
import numpy as np

D_MODEL = 1024
BATCH = 2
SEQ = 8192
DEPTH = 4
DEC_BATCH = 32
DEC_SEQ = 4
PAST_LEN = 8192
PAGE_SIZE = 128

HEAD_DIM = 128
N_MEM = 256
ML_HEADS = 4
ML_W = ML_HEADS * HEAD_DIM
ML_CHUNK = 128
GM_GROUPS = 4
GM_GW = 128
GM_W = GM_GROUPS * GM_GW
GM_CHUNK = 128
AT_HEADS = 8
AT_KV_HEADS = 2
AT_W = AT_HEADS * HEAD_DIM
KV_W = AT_KV_HEADS * HEAD_DIM
IDX_HEADS = 8
IDX_DIM = 64
TOPK_MAX = 256
Q_BLOCK = 128
XA_HEADS = 4
XA_W = XA_HEADS * HEAD_DIM
ROPE_THETA = 500000.0
ROPE_FRAC = 4
NORM_EPS = 1e-6
N_EVEN = (DEPTH + 1) // 2
N_ODD = DEPTH // 2

EVEN_SPLITS = (ML_W, ML_W, ML_W, ML_W, ML_HEADS, ML_HEADS, ML_W,
               GM_W, GM_W, GM_W,
               XA_W, XA_W)
ODD_SPLITS = (AT_W, KV_W, KV_W, AT_W,
              IDX_HEADS * IDX_DIM, IDX_HEADS, IDX_DIM,
              XA_W, XA_W)
EVEN_IN = sum(EVEN_SPLITS)
ODD_IN = sum(ODD_SPLITS)
EVEN_OUT = ML_W + GM_W + XA_W
ODD_OUT = AT_W + XA_W

kernel_name = 'hybrid_mlstm_gmlp_dsa_decoder_step'


def _rms(x, g):
    xf = x.astype(jnp.float32)
    y = xf * lax.rsqrt(jnp.mean(xf * xf, axis=-1, keepdims=True) + NORM_EPS)
    return (y * g.astype(jnp.float32)).astype(x.dtype)


def _layernorm(x, g, b):
    xf = x.astype(jnp.float32)
    xc = xf - jnp.mean(xf, axis=-1, keepdims=True)
    var = jnp.mean(xc * xc, axis=-1, keepdims=True)
    y = xc * lax.rsqrt(var + NORM_EPS) * g.astype(jnp.float32) + b.astype(jnp.float32)
    return y.astype(x.dtype)


def _split(h, sizes):
    cuts = np.cumsum(np.asarray(sizes))[:-1].tolist()
    return jnp.split(h, cuts, axis=-1)


def _rope(x, pos):
    rd = x.shape[-1] // ROPE_FRAC
    half = rd // 2
    inv = ROPE_THETA ** (-jnp.arange(half, dtype=jnp.float32) / half)
    ang = pos.astype(jnp.float32)[:, None] * inv[None, :]
    cos = jnp.cos(ang)[:, None, :]
    sin = jnp.sin(ang)[:, None, :]
    xf = x.astype(jnp.float32)
    x1, x2 = xf[..., :half], xf[..., half:rd]
    out = jnp.concatenate([x1 * cos - x2 * sin, x2 * cos + x1 * sin, xf[..., rd:]], axis=-1)
    return out.astype(x.dtype)


def _mlstm(q, k, v, log_i, log_f, C0, n0, m0):
    B, T, H, Dh = q.shape
    L = math.gcd(T, ML_CHUNK)
    nc = T // L

    def to_chunks(a):
        return jnp.moveaxis(a.reshape((B, nc, L) + a.shape[2:]), 1, 0)

    causal = jnp.tril(jnp.ones((L, L), dtype=bool))

    def step(carry, inp):
        C, n, m = carry
        qc, kc, vc, li, lf = inp
        b = jnp.cumsum(lf, axis=1)
        dlog = b[:, :, None, :] - b[:, None, :, :] + li[:, None, :, :]
        dlog = jnp.where(causal[None, :, :, None], dlog, -jnp.inf)
        inter = b + m[:, None, :]
        m_t = jnp.maximum(inter, jnp.max(dlog, axis=2))
        w_intra = jnp.exp(dlog - m_t[:, :, None, :])
        w_inter = jnp.exp(inter - m_t)
        qk = jnp.einsum('bjhd,bshd->bjsh', qc, kc) * w_intra
        num = jnp.einsum('bjsh,bshd->bjhd', qk, vc) + w_inter[..., None] * jnp.einsum('bjhd,bhde->bjhe', qc, C)
        den = jnp.sum(qk, axis=2) + w_inter * jnp.einsum('bjhd,bhd->bjh', qc, n)
        h = num / jnp.maximum(jnp.abs(den), jnp.exp(-m_t))[..., None]
        b_end = b[:, -1, :]
        g = b_end[:, None, :] - b + li
        m_new = jnp.maximum(b_end + m, jnp.max(g, axis=1))
        a_prev = jnp.exp(b_end + m - m_new)
        a_tok = jnp.exp(g - m_new[:, None, :])
        C_new = a_prev[..., None, None] * C + jnp.einsum('bsh,bshd,bshe->bhde', a_tok, kc, vc)
        n_new = a_prev[..., None] * n + jnp.einsum('bsh,bshd->bhd', a_tok, kc)
        return (C_new, n_new, m_new), h

    init = (C0.astype(jnp.float32), n0.astype(jnp.float32), m0.astype(jnp.float32))
    (C, n, m), hs = lax.scan(step, init, tuple(map(to_chunks, (q, k, v, log_i, log_f))))
    return jnp.moveaxis(hs, 0, 1).reshape(B, T, H, Dh), C, n, m


def _gmlp(u, v, w_s, b_s):
    B, T, _ = v.shape
    nck = max(T // GM_CHUNK, 1)
    c = T // nck
    w = jnp.where(jnp.tril(jnp.ones((c, c), dtype=bool))[None], w_s[:, :c, :c], 0.0)
    vg = v.reshape(B, nck, c, GM_GROUPS, GM_GW)
    mixed = jnp.einsum('gts,bnsgc->bntgc', w, vg) + b_s[:, :c].T[None, None, :, :, None]
    return u * mixed.reshape(B, T, GM_W).astype(u.dtype)


def _mem_kv(mem, w_kv, kn):
    B, M, _ = mem.shape
    mk, mv = _split(mem @ w_kv, (XA_W, XA_W))
    mk = _rms(mk.reshape(B, M, XA_HEADS, HEAD_DIM), kn)
    return mk, mv.reshape(B, M, XA_HEADS, HEAD_DIM)


def _mem_attn(xq, mk, mv, qn):
    q = _rms(xq, qn)
    s = jnp.einsum('bthd,bmhd->bhtm', q, mk).astype(jnp.float32) * HEAD_DIM ** -0.5
    p = jax.nn.softmax(s, axis=-1).astype(mv.dtype)
    return jnp.einsum('bhtm,bmhd->bthd', p, mv)


def _index_select(qi, wi, ki, qpos, topk):
    s = jnp.einsum('bqhd,bld->bqhl', qi.astype(jnp.float32), ki.astype(jnp.float32)) * IDX_DIM ** -0.5
    s = jnp.einsum('bqhl,bqh->bql', jax.nn.relu(s), wi.astype(jnp.float32))
    kpos = jnp.arange(ki.shape[1])
    s = jnp.where((kpos[None, :] <= qpos[:, None])[None], s, -jnp.inf)
    _, idx = lax.top_k(s, topk)
    return idx, idx <= qpos[None, :, None]


def _sparse_attend(q, kg, vg, valid):
    B, Q, H, Dh = q.shape
    qg = q.reshape(B, Q, AT_KV_HEADS, H // AT_KV_HEADS, Dh)
    s = jnp.einsum('bqngd,bqknd->bqngk', qg, kg).astype(jnp.float32) * Dh ** -0.5
    s = jnp.where(valid[:, :, None, None, :], s, -jnp.inf)
    p = jax.nn.softmax(s, axis=-1).astype(vg.dtype)
    return jnp.einsum('bqngk,bqknd->bqngd', p, vg).reshape(B, Q, H, Dh)


def _dsa_prompt(q, k, v, qi, wi, ki):
    B, S = q.shape[:2]
    topk = min(TOPK_MAX, S // 4)
    take = jax.vmap(lambda rows, idx: rows[idx])

    def block(i):
        s0 = i * Q_BLOCK
        sl = lambda a: lax.dynamic_slice_in_dim(a, s0, Q_BLOCK, axis=1)
        qpos = s0 + jnp.arange(Q_BLOCK)
        idx, valid = _index_select(sl(qi), sl(wi), ki, qpos, topk)
        return _sparse_attend(sl(q), take(k, idx), take(v, idx), valid)

    out = lax.map(block, jnp.arange(S // Q_BLOCK))
    return jnp.moveaxis(out, 0, 1).reshape(B, S, AT_HEADS, HEAD_DIM)


def _paged_take(pool, layer, page_table, idx, new_rows):
    past = page_table.shape[1] * PAGE_SIZE
    pidx = jnp.minimum(idx, past - 1)
    phys = jax.vmap(lambda pt, pi: pt[pi])(page_table, pidx // PAGE_SIZE)
    rows_past = pool[layer, phys, pidx % PAGE_SIZE].astype(new_rows.dtype)
    nidx = jnp.clip(idx - past, 0, new_rows.shape[1] - 1)
    rows_new = jax.vmap(lambda r, i: r[i])(new_rows, nidx)
    in_past = idx < past
    mask = in_past.reshape(in_past.shape + (1,) * (rows_new.ndim - in_past.ndim))
    return jnp.where(mask, rows_past, rows_new)


def _dsa_sample(q, k, v, qi, wi, ki, layer, cache_k, cache_v, cache_ik, page_table):
    Bd, T = q.shape[:2]
    past = page_table.shape[1] * PAGE_SIZE
    topk = min(TOPK_MAX, (past + T) // 4)
    ki_past = cache_ik[layer, page_table].reshape(Bd, past, IDX_DIM).astype(ki.dtype)
    ki_all = jnp.concatenate([ki_past, ki], axis=1)
    qpos = past + jnp.arange(T)
    idx, valid = _index_select(qi, wi, ki_all, qpos, topk)
    kg = _paged_take(cache_k, layer, page_table, idx, k)
    vg = _paged_take(cache_v, layer, page_table, idx, v)
    return _sparse_attend(q, kg, vg, valid)


def _even_layer(x, mk, mv, C0, n0, m0, norm_g, w_in, b_if, h_g, ln_g, ln_b, w_s, b_s, w_out, mem_qn):
    B, T, _ = x.shape
    xn = _rms(x, norm_g)
    q, k, v, o, ig, fg, g_a, u, vb, g_b, xq, g_x = _split(xn @ w_in, EVEN_SPLITS)
    heads = lambda a, n: a.reshape(B, T, n, HEAD_DIM)
    f32 = jnp.float32
    log_i = (ig + b_if[:ML_HEADS]).astype(f32)
    log_f = jax.nn.log_sigmoid((fg + b_if[ML_HEADS:]).astype(f32))
    hA, C, n, m = _mlstm(heads(q, ML_HEADS).astype(f32),
                         heads(k, ML_HEADS).astype(f32) * HEAD_DIM ** -0.5,
                         heads(v, ML_HEADS).astype(f32), log_i, log_f, C0, n0, m0)
    hA = _rms(hA, h_g.reshape(ML_HEADS, HEAD_DIM)).reshape(B, T, ML_W).astype(x.dtype)
    yA = hA * jax.nn.sigmoid(o) * jax.nn.silu(g_a)
    vbn = _layernorm(vb, ln_g, ln_b)
    yB = _gmlp(u, vbn, w_s, b_s) * jax.nn.silu(g_b)
    yX = _mem_attn(heads(xq, XA_HEADS), mk, mv, mem_qn).reshape(B, T, XA_W) * jax.nn.silu(g_x)
    y = jnp.concatenate([yA, yB, yX.astype(yA.dtype)], axis=-1) @ w_out
    return x + y.astype(x.dtype), (C, n, m), vbn


def _odd_layer(x, pos, mk, mv, attend, norm_g, w_in, qn, kn, w_out, mem_qn):
    B, T, _ = x.shape
    xn = _rms(x, norm_g)
    q, k, v, g_c, qi, wi, ki, xq, g_x = _split(xn @ w_in, ODD_SPLITS)
    heads = lambda a, n: a.reshape(B, T, n, HEAD_DIM)
    q = _rope(_rms(heads(q, AT_HEADS), qn), pos)
    k = _rope(_rms(heads(k, AT_KV_HEADS), kn), pos)
    v = heads(v, AT_KV_HEADS)
    qi = _rope(qi.reshape(B, T, IDX_HEADS, IDX_DIM), pos)
    ki = _rope(ki[:, :, None, :], pos)[:, :, 0, :]
    wi = wi * IDX_HEADS ** -0.5
    yC = attend(q, k, v, qi, wi, ki).reshape(B, T, AT_W) * jax.nn.silu(g_c)
    yX = _mem_attn(heads(xq, XA_HEADS), mk, mv, mem_qn).reshape(B, T, XA_W) * jax.nn.silu(g_x)
    y = jnp.concatenate([yC, yX.astype(yC.dtype)], axis=-1) @ w_out
    return x + y.astype(x.dtype), (k, v, ki)


def setup_inputs(seed: int = 0) -> dict:
    key = jax.random.key(seed)
    keys = jax.random.split(key, 32)

    def nrm(i, shape, scale=1.0):
        return scale * jax.random.normal(keys[i], shape, jnp.float32)

    n_pages = PAST_LEN // PAGE_SIZE
    n_used = DEC_BATCH * n_pages
    n_pool = n_used + max(1, n_used // 4)
    page_table = jax.random.permutation(keys[10], n_pool)[:n_used].reshape(DEC_BATCH, n_pages).astype(jnp.int32)
    b_if = jnp.concatenate([nrm(14, (N_EVEN, ML_HEADS), 0.1),
                            3.0 + nrm(15, (N_EVEN, ML_HEADS), 0.5)], axis=-1)
    return {
        'x_prompt': nrm(0, (BATCH, SEQ, D_MODEL)),
        'x_sample': nrm(1, (DEC_BATCH, DEC_SEQ, D_MODEL)),
        'state_mlstm_C': nrm(2, (N_EVEN, DEC_BATCH, ML_HEADS, HEAD_DIM, HEAD_DIM), 0.05),
        'state_mlstm_n': nrm(3, (N_EVEN, DEC_BATCH, ML_HEADS, HEAD_DIM), 0.05),
        'state_mlstm_m': nrm(4, (N_EVEN, DEC_BATCH, ML_HEADS), 0.5),
        'cache_attn_k': nrm(5, (N_ODD, n_pool, PAGE_SIZE, AT_KV_HEADS, HEAD_DIM)),
        'cache_attn_v': nrm(6, (N_ODD, n_pool, PAGE_SIZE, AT_KV_HEADS, HEAD_DIM)),
        'cache_idx_k': nrm(7, (N_ODD, n_pool, PAGE_SIZE, IDX_DIM)),
        'cache_mem_k': nrm(8, (DEPTH, DEC_BATCH, N_MEM, XA_HEADS, HEAD_DIM)),
        'cache_mem_v': nrm(9, (DEPTH, DEC_BATCH, N_MEM, XA_HEADS, HEAD_DIM)),
        'page_table': page_table,
        'mem_prompt': nrm(11, (BATCH, N_MEM, D_MODEL)),
        'norm_even': 1.0 + nrm(12, (N_EVEN, D_MODEL), 0.02),
        'w_in_even': nrm(13, (N_EVEN, D_MODEL, EVEN_IN), D_MODEL ** -0.5),
        'b_if_even': b_if,
        'mlstm_hnorm': 1.0 + nrm(16, (N_EVEN, ML_W), 0.02),
        'gmlp_ln_g': 1.0 + nrm(17, (N_EVEN, GM_W), 0.02),
        'gmlp_ln_b': nrm(18, (N_EVEN, GM_W), 0.02),
        'gmlp_w': nrm(19, (N_EVEN, GM_GROUPS, GM_CHUNK, GM_CHUNK), GM_CHUNK ** -0.5),
        'gmlp_b': nrm(20, (N_EVEN, GM_GROUPS, GM_CHUNK), 0.1),
        'w_out_even': nrm(21, (N_EVEN, EVEN_OUT, D_MODEL), EVEN_OUT ** -0.5),
        'norm_odd': 1.0 + nrm(22, (N_ODD, D_MODEL), 0.02),
        'w_in_odd': nrm(23, (N_ODD, D_MODEL, ODD_IN), D_MODEL ** -0.5),
        'attn_qn': 1.0 + nrm(24, (N_ODD, HEAD_DIM), 0.02),
        'attn_kn': 1.0 + nrm(25, (N_ODD, HEAD_DIM), 0.02),
        'w_out_odd': nrm(26, (N_ODD, ODD_OUT, D_MODEL), ODD_OUT ** -0.5),
        'w_mem_kv': nrm(27, (DEPTH, D_MODEL, 2 * XA_W), D_MODEL ** -0.5),
        'mem_qn': 1.0 + nrm(28, (DEPTH, HEAD_DIM), 0.02),
        'mem_kn': 1.0 + nrm(29, (DEPTH, HEAD_DIM), 0.02),
    }


def reference(x_prompt, x_sample, state_mlstm_C, state_mlstm_n, state_mlstm_m,
              cache_attn_k, cache_attn_v, cache_idx_k, cache_mem_k, cache_mem_v,
              page_table, mem_prompt,
              norm_even, w_in_even, b_if_even, mlstm_hnorm, gmlp_ln_g, gmlp_ln_b,
              gmlp_w, gmlp_b, w_out_even,
              norm_odd, w_in_odd, attn_qn, attn_kn, w_out_odd,
              w_mem_kv, mem_qn, mem_kn):
    B, S, _ = x_prompt.shape
    Bd, T, _ = x_sample.shape
    past = page_table.shape[1] * PAGE_SIZE
    pos_p = jnp.arange(S)
    pos_s = past + jnp.arange(T)
    zero_C = jnp.zeros((B, ML_HEADS, HEAD_DIM, HEAD_DIM), jnp.float32)
    zero_n = jnp.zeros((B, ML_HEADS, HEAD_DIM), jnp.float32)
    zero_m = jnp.zeros((B, ML_HEADS), jnp.float32)
    xp, xs = x_prompt, x_sample
    mlC_p, mln_p, mlm_p, mlC_s, mln_s, mlm_s, gv_s = [], [], [], [], [], [], []
    ak_p, av_p, ik_p, ak_s, av_s, ik_s, memk_p, memv_p = [], [], [], [], [], [], [], []
    for l in range(DEPTH):
        mk_p, mv_p = _mem_kv(mem_prompt, w_mem_kv[l], mem_kn[l])
        memk_p.append(mk_p)
        memv_p.append(mv_p)
        if l % 2 == 0:
            e = l // 2
            prm = (norm_even[e], w_in_even[e], b_if_even[e], mlstm_hnorm[e], gmlp_ln_g[e], gmlp_ln_b[e],
                   gmlp_w[e], gmlp_b[e], w_out_even[e], mem_qn[l])
            xp, (c, n, m), _ = _even_layer(xp, mk_p, mv_p, zero_C, zero_n, zero_m, *prm)
            mlC_p.append(c)
            mln_p.append(n)
            mlm_p.append(m)
            xs, (c, n, m), vrows = _even_layer(xs, cache_mem_k[l], cache_mem_v[l], state_mlstm_C[e],
                                               state_mlstm_n[e], state_mlstm_m[e], *prm)
            mlC_s.append(c)
            mln_s.append(n)
            mlm_s.append(m)
            gv_s.append(vrows)
        else:
            o = l // 2
            prm = (norm_odd[o], w_in_odd[o], attn_qn[o], attn_kn[o], w_out_odd[o], mem_qn[l])
            xp, (k, v, ki) = _odd_layer(xp, pos_p, mk_p, mv_p, _dsa_prompt, *prm)
            ak_p.append(k)
            av_p.append(v)
            ik_p.append(ki)
            attend_s = functools.partial(_dsa_sample, layer=o, cache_k=cache_attn_k, cache_v=cache_attn_v,
                                         cache_ik=cache_idx_k, page_table=page_table)
            xs, (k, v, ki) = _odd_layer(xs, pos_s, cache_mem_k[l], cache_mem_v[l], attend_s, *prm)
            ak_s.append(k)
            av_s.append(v)
            ik_s.append(ki)
    return (xp, xs,
            jnp.stack(mlC_p), jnp.stack(mln_p), jnp.stack(mlm_p),
            jnp.stack(mlC_s), jnp.stack(mln_s), jnp.stack(mlm_s),
            jnp.stack(gv_s),
            jnp.stack(ak_p), jnp.stack(av_p), jnp.stack(ik_p),
            jnp.stack(ak_s), jnp.stack(av_s), jnp.stack(ik_s),
            jnp.stack(memk_p), jnp.stack(memv_p))
```

```python
import functools
import math

import numpy as np
import jax
import jax.numpy as jnp
from jax import lax
from jax.experimental import pallas as pl
from jax.experimental.pallas import tpu as pltpu

F32 = jnp.float32
BF16 = jnp.bfloat16
I32 = jnp.int32

HEAD_DIM = 128
ML_HEADS = 4
GM_GROUPS = 4
AT_HEADS = 8
AT_KV_HEADS = 2
AT_GROUP = AT_HEADS // AT_KV_HEADS
IDX_HEADS = 8
IDX_DIM = 64
XA_HEADS = 4
TOPK_MAX = 256
PAGE_SIZE = 128
ROPE_THETA = 500000.0
ROPE_FRAC = 4
NORM_EPS = 1e-6

ML_W = ML_HEADS * HEAD_DIM
GM_W = GM_GROUPS * HEAD_DIM
XA_W = XA_HEADS * HEAD_DIM
AT_W = AT_HEADS * HEAD_DIM
KV_W = AT_KV_HEADS * HEAD_DIM
IDX_W = IDX_HEADS * IDX_DIM

LANES = 128
SUBLANES = 8
VMEM_LIMIT = 56 * 1024 * 1024
NEG = -1e30
INT_MIN = -2 ** 31
SAMPLE_PAD = SUBLANES
PAGES_PER_STEP = 8

E_Q, E_K, E_V, E_O, E_GA, E_U, E_VB, E_GB, E_XQ, E_GX, E_IF = (
    0, 512, 1024, 1536, 2048, 2560, 3072, 3584, 4096, 4608, 5120)
E_MAIN = 5120
E_COLS = E_MAIN + LANES
O_Q, O_GC, O_K, O_V, O_QI, O_XQ, O_GX, O_KI = 0, 1024, 2048, 2304, 2560, 3072, 3584, 4096
O_MAIN = 4096
O_COLS = O_MAIN + LANES


def _cparams(sem):
    return pltpu.CompilerParams(dimension_semantics=sem, vmem_limit_bytes=VMEM_LIMIT)


def _sigmoid(x):
    return 1.0 / (1.0 + jnp.exp(-x))


def _silu(x):
    return x * _sigmoid(x)


def _rms_rows(x, g):
    return x * lax.rsqrt(jnp.mean(x * x, axis=-1, keepdims=True) + NORM_EPS) * g


def _dot_nt(a, b):
    return lax.dot_general(a, b, (((1,), (1,)), ((), ())), preferred_element_type=F32)


def _dot_tn(a, b):
    return lax.dot_general(a, b, (((0,), (0,)), ((), ())), preferred_element_type=F32)


def _dot(a, b):
    return jnp.dot(a, b, preferred_element_type=F32)


def _in_proj_kernel(x_ref, g_ref, w_ref, o_ref, *, col_chunk):
    xn = _rms_rows(x_ref[...], g_ref[...]).astype(BF16)
    n = o_ref.shape[-1]
    for c0 in range(0, n, col_chunk):
        c1 = min(c0 + col_chunk, n)
        o_ref[:, c0:c1] = _dot(xn, w_ref[:, c0:c1])


def _in_proj(x2, g, w_bf, tm):
    m, d = x2.shape
    n = w_bf.shape[1]
    return pl.pallas_call(
        functools.partial(_in_proj_kernel, col_chunk=512),
        grid=(m // tm,),
        in_specs=[pl.BlockSpec((tm, d), lambda i: (i, 0)),
                  pl.BlockSpec((1, d), lambda i: (0, 0)),
                  pl.BlockSpec((d, n), lambda i: (0, 0))],
        out_specs=pl.BlockSpec((tm, n), lambda i: (i, 0)),
        out_shape=jax.ShapeDtypeStruct((m, n), F32),
        compiler_params=_cparams(("parallel",)),
        name="in_proj",
    )(x2, g.reshape(1, d), w_bf)


def _out_proj_kernel(*refs, n_in):
    x_ref = refs[0]
    y_refs = refs[1:1 + n_in]
    w_refs = refs[1 + n_in:1 + 2 * n_in]
    o_ref = refs[1 + 2 * n_in]
    acc = x_ref[...]
    for y_ref, w_ref in zip(y_refs, w_refs):
        acc = acc + _dot(y_ref[...].astype(BF16), w_ref[...])
    o_ref[...] = acc


def _out_proj(x2, ys, ws, tm):
    m, d = x2.shape
    n_in = len(ys)
    in_specs = [pl.BlockSpec((tm, d), lambda i: (i, 0))]
    in_specs += [pl.BlockSpec((tm, y.shape[1]), lambda i: (i, 0)) for y in ys]
    in_specs += [pl.BlockSpec(w.shape, lambda i: (0, 0)) for w in ws]
    return pl.pallas_call(
        functools.partial(_out_proj_kernel, n_in=n_in),
        grid=(m // tm,),
        in_specs=in_specs,
        out_specs=pl.BlockSpec((tm, d), lambda i: (i, 0)),
        out_shape=jax.ShapeDtypeStruct((m, d), F32),
        compiler_params=_cparams(("parallel",)),
        name="out_proj",
    )(x2, *ys, *ws)


def _mem_kv_kernel(mem_ref, w_ref, kn_ref, k_ref, v_ref):
    kv = _dot(mem_ref[...].astype(BF16), w_ref[...])
    kn = kn_ref[...]
    for h in range(XA_HEADS):
        sl = slice(h * HEAD_DIM, (h + 1) * HEAD_DIM)
        k_ref[:, sl] = _rms_rows(kv[:, sl], kn)
    v_ref[...] = kv[:, XA_W:]


def _mem_kv(mem2, w_kv_bf, mem_kn, tm):
    m, d = mem2.shape
    depth = w_kv_bf.shape[0]
    out = jax.ShapeDtypeStruct((depth, m, XA_W), F32)
    return pl.pallas_call(
        _mem_kv_kernel,
        grid=(depth, m // tm),
        in_specs=[pl.BlockSpec((tm, d), lambda l, i: (i, 0)),
                  pl.BlockSpec((None, d, 2 * XA_W), lambda l, i: (l, 0, 0)),
                  pl.BlockSpec((None, 1, HEAD_DIM), lambda l, i: (l, 0, 0))],
        out_specs=[pl.BlockSpec((None, tm, XA_W), lambda l, i: (l, i, 0)),
                   pl.BlockSpec((None, tm, XA_W), lambda l, i: (l, i, 0))],
        out_shape=[out, out],
        compiler_params=_cparams(("parallel", "parallel")),
        name="mem_kv",
    )(mem2, w_kv_bf, mem_kn.reshape(depth, 1, HEAD_DIM))


def _mem_attn_heads(xq, gx, mk_ref, mv_ref, qn):
    outs = []
    for h in range(XA_HEADS):
        sl = slice(h * HEAD_DIM, (h + 1) * HEAD_DIM)
        q = _rms_rows(xq[:, sl], qn).astype(BF16)
        s = _dot_nt(q, mk_ref[:, sl].astype(BF16)) * HEAD_DIM ** -0.5
        s = s - jnp.max(s, axis=-1, keepdims=True)
        p = jnp.exp(s)
        p = p / jnp.sum(p, axis=-1, keepdims=True)
        o = _dot(p.astype(BF16), mv_ref[:, sl].astype(BF16))
        outs.append(o * _silu(gx[:, sl]))
    return outs


def _even_mix_kernel(h_ref, if_ref, c0_ref, n0_ref, m0_ref, bif_ref, hg_ref, lng_ref, lnb_ref,
                     ws_ref, bs_ref, mk_ref, mv_ref, qn_ref,
                     y_ref, vbn_ref, c_ref, n_ref, m_ref, *, t_valid):
    L = h_ref.shape[0]
    ci = pl.program_id(1)

    @pl.when(ci == 0)
    def _():
        c_ref[...] = c0_ref[...]
        n_ref[...] = n0_ref[...]
        m_ref[...] = m0_ref[...]

    row = lax.broadcasted_iota(I32, (L, L), 0)
    col = lax.broadcasted_iota(I32, (L, L), 1)
    eye = row == col
    row_le_col = row <= col
    col_le_row = col <= row
    tcol = lax.broadcasted_iota(I32, (L, 1), 0)

    gates = if_ref[...] + bif_ref[...]
    log_f_all = jnp.minimum(gates, 0.0) - jnp.log(1.0 + jnp.exp(-jnp.abs(gates)))

    for h in range(ML_HEADS):
        sl = slice(h * HEAD_DIM, (h + 1) * HEAD_DIM)
        li_col = gates[:, h:h + 1]
        lf_col = log_f_all[:, ML_HEADS + h:ML_HEADS + h + 1]
        if t_valid < L:
            li_col = jnp.where(tcol < t_valid, li_col, NEG)
            lf_col = jnp.where(tcol < t_valid, lf_col, 0.0)
        li_b = jnp.broadcast_to(li_col, (L, L))
        lf_b = jnp.broadcast_to(lf_col, (L, L))
        li_row = jnp.sum(jnp.where(eye, li_b, 0.0), axis=0, keepdims=True)
        lf_row = jnp.sum(jnp.where(eye, lf_b, 0.0), axis=0, keepdims=True)
        b_row = jnp.sum(jnp.where(row_le_col, lf_b, 0.0), axis=0, keepdims=True)
        b_col = jnp.sum(jnp.where(col_le_row, jnp.broadcast_to(lf_row, (L, L)), 0.0),
                        axis=1, keepdims=True)
        m_prev = m_ref[h:h + 1, 0:1]
        n_prev = n_ref[h:h + 1, :]
        c_prev = c_ref[h]

        dlog = jnp.where(col_le_row, b_col - b_row + li_row, NEG)
        inter = b_col + m_prev
        m_t = jnp.maximum(inter, jnp.max(dlog, axis=1, keepdims=True))
        w_intra = jnp.exp(dlog - m_t)
        w_inter = jnp.exp(inter - m_t)
        q = h_ref[:, E_Q + h * HEAD_DIM:E_Q + (h + 1) * HEAD_DIM]
        k = h_ref[:, E_K + h * HEAD_DIM:E_K + (h + 1) * HEAD_DIM] * HEAD_DIM ** -0.5
        v = h_ref[:, E_V + h * HEAD_DIM:E_V + (h + 1) * HEAD_DIM]
        q_bf = q.astype(BF16)
        k_bf = k.astype(BF16)
        qk = _dot_nt(q_bf, k_bf) * w_intra
        num = _dot(qk.astype(BF16), v.astype(BF16)) + w_inter * _dot(q_bf, c_prev.astype(BF16))
        den = jnp.sum(qk, axis=1, keepdims=True) + w_inter * jnp.sum(q * n_prev, axis=1, keepdims=True)
        hh = num / jnp.maximum(jnp.abs(den), jnp.exp(-m_t))

        b_end = b_col[L - 1:L, :]
        g = b_end - b_col + li_col
        m_new = jnp.maximum(b_end + m_prev, jnp.max(g, axis=0, keepdims=True))
        a_prev = jnp.exp(b_end + m_prev - m_new)
        a_tok = jnp.exp(g - m_new)
        c_ref[h] = a_prev * c_prev + _dot_tn(k_bf, (a_tok * v).astype(BF16))
        n_ref[h:h + 1, :] = a_prev * n_prev + jnp.sum(a_tok * k, axis=0, keepdims=True)
        m_ref[h:h + 1, :] = jnp.broadcast_to(m_new, (1, LANES))

        ha = _rms_rows(hh, hg_ref[:, sl])
        y_ref[:, sl] = ha * _sigmoid(h_ref[:, E_O + h * HEAD_DIM:E_O + (h + 1) * HEAD_DIM]) * _silu(
            h_ref[:, E_GA + h * HEAD_DIM:E_GA + (h + 1) * HEAD_DIM])

    vb = h_ref[:, E_VB:E_VB + GM_W]
    vc = vb - jnp.mean(vb, axis=-1, keepdims=True)
    var = jnp.mean(vc * vc, axis=-1, keepdims=True)
    vbn = vc * lax.rsqrt(var + NORM_EPS) * lng_ref[...] + lnb_ref[...]
    vbn_ref[...] = vbn
    for g_i in range(GM_GROUPS):
        sl = slice(g_i * HEAD_DIM, (g_i + 1) * HEAD_DIM)
        w = jnp.where(col_le_row, ws_ref[g_i], 0.0).astype(BF16)
        mixed = _dot(w, vbn[:, sl].astype(BF16)) + bs_ref[:, g_i:g_i + 1]
        y_ref[:, ML_W + g_i * HEAD_DIM:ML_W + (g_i + 1) * HEAD_DIM] = (
            h_ref[:, E_U + g_i * HEAD_DIM:E_U + (g_i + 1) * HEAD_DIM] * mixed
            * _silu(h_ref[:, E_GB + g_i * HEAD_DIM:E_GB + (g_i + 1) * HEAD_DIM]))

    outs = _mem_attn_heads(h_ref[:, E_XQ:E_XQ + XA_W], h_ref[:, E_GX:E_GX + XA_W], mk_ref, mv_ref, qn_ref[...])
    for h, o in enumerate(outs):
        y_ref[:, ML_W + GM_W + h * HEAD_DIM:ML_W + GM_W + (h + 1) * HEAD_DIM] = o


def _even_mix(h3, c0, n0, m0b, bif, hg, lng, lnb, ws, bs_t, mk, mv, qn, L, t_valid):
    b, t, _ = h3.shape
    nc = t // L
    n_mem = mk.shape[1]
    full2 = lambda bi, ci: (0, 0)
    return pl.pallas_call(
        functools.partial(_even_mix_kernel, t_valid=t_valid),
        grid=(b, nc),
        in_specs=[pl.BlockSpec((None, L, E_MAIN), lambda bi, ci: (bi, ci, 0)),
                  pl.BlockSpec((None, L, LANES), lambda bi, ci: (bi, ci, E_MAIN // LANES)),
                  pl.BlockSpec((None, ML_HEADS, HEAD_DIM, HEAD_DIM), lambda bi, ci: (bi, 0, 0, 0)),
                  pl.BlockSpec((None, ML_HEADS, HEAD_DIM), lambda bi, ci: (bi, 0, 0)),
                  pl.BlockSpec((None, ML_HEADS, LANES), lambda bi, ci: (bi, 0, 0)),
                  pl.BlockSpec((1, LANES), full2),
                  pl.BlockSpec((1, ML_W), full2),
                  pl.BlockSpec((1, GM_W), full2),
                  pl.BlockSpec((1, GM_W), full2),
                  pl.BlockSpec((GM_GROUPS, L, L), lambda bi, ci: (0, 0, 0)),
                  pl.BlockSpec((L, LANES), full2),
                  pl.BlockSpec((None, n_mem, XA_W), lambda bi, ci: (bi, 0, 0)),
                  pl.BlockSpec((None, n_mem, XA_W), lambda bi, ci: (bi, 0, 0)),
                  pl.BlockSpec((1, HEAD_DIM), full2)],
        out_specs=[pl.BlockSpec((None, L, ML_W + GM_W + XA_W), lambda bi, ci: (bi, ci, 0)),
                   pl.BlockSpec((None, L, GM_W), lambda bi, ci: (bi, ci, 0)),
                   pl.BlockSpec((None, ML_HEADS, HEAD_DIM, HEAD_DIM), lambda bi, ci: (bi, 0, 0, 0)),
                   pl.BlockSpec((None, ML_HEADS, HEAD_DIM), lambda bi, ci: (bi, 0, 0)),
                   pl.BlockSpec((None, ML_HEADS, LANES), lambda bi, ci: (bi, 0, 0))],
        out_shape=[jax.ShapeDtypeStruct((b, t, ML_W + GM_W + XA_W), F32),
                   jax.ShapeDtypeStruct((b, t, GM_W), F32),
                   jax.ShapeDtypeStruct((b, ML_HEADS, HEAD_DIM, HEAD_DIM), F32),
                   jax.ShapeDtypeStruct((b, ML_HEADS, HEAD_DIM), F32),
                   jax.ShapeDtypeStruct((b, ML_HEADS, LANES), F32)],
        compiler_params=_cparams(("parallel", "arbitrary")),
        name="even_mix",
    )(h3, h3, c0, n0, m0b, bif, hg, lng, lnb, ws, bs_t, mk, mv, qn)


def _rope_lanes(x, cos, sin_lo, sin_hi, half):
    n = x.shape[-1]
    return x * cos + pltpu.roll(x, n - half, 1) * sin_lo + pltpu.roll(x, half, 1) * sin_hi


def _odd_pre_kernel(h_ref, kiwi_ref, ca_ref, sa1_ref, sa2_ref, ci_ref, si1_ref, si2_ref,
                    qn_ref, kn_ref, mk_ref, mv_ref, mqn_ref,
                    q_ref, k_ref, v_ref, kbf_ref, vbf_ref, qi_ref, ki_ref, kibf_ref, wi_ref, yx_ref):
    ca, sa1, sa2 = ca_ref[...], sa1_ref[...], sa2_ref[...]
    ci, si1, si2 = ci_ref[...], si1_ref[...], si2_ref[...]
    a_half = HEAD_DIM // ROPE_FRAC // 2
    i_half = IDX_DIM // ROPE_FRAC // 2
    for h in range(AT_HEADS):
        sl = slice(O_Q + h * HEAD_DIM, O_Q + (h + 1) * HEAD_DIM)
        qh = _rope_lanes(_rms_rows(h_ref[:, sl], qn_ref[...]), ca, sa1, sa2, a_half)
        q_ref[:, h * HEAD_DIM:(h + 1) * HEAD_DIM] = qh.astype(BF16)
    for h in range(AT_KV_HEADS):
        sl = slice(O_K + h * HEAD_DIM, O_K + (h + 1) * HEAD_DIM)
        kh = _rope_lanes(_rms_rows(h_ref[:, sl], kn_ref[...]), ca, sa1, sa2, a_half)
        k_ref[:, h * HEAD_DIM:(h + 1) * HEAD_DIM] = kh
        kbf_ref[:, h * HEAD_DIM:(h + 1) * HEAD_DIM] = kh.astype(BF16)
    v = h_ref[:, O_V:O_V + KV_W]
    v_ref[...] = v
    vbf_ref[...] = v.astype(BF16)
    for j in range(IDX_W // LANES):
        sl = slice(O_QI + j * LANES, O_QI + (j + 1) * LANES)
        qi_ref[:, j * LANES:(j + 1) * LANES] = _rope_lanes(h_ref[:, sl], ci, si1, si2, i_half).astype(BF16)
    kiwi = kiwi_ref[...]
    ki = _rope_lanes(kiwi, ci, si1, si2, i_half)[:, :IDX_DIM]
    ki_ref[...] = ki
    kibf_ref[...] = ki.astype(BF16)
    lane = lax.broadcasted_iota(I32, kiwi.shape, 1)
    wsel = (lane >= IDX_DIM) & (lane < IDX_DIM + IDX_HEADS)
    wi_ref[...] = jnp.where(wsel, kiwi * IDX_HEADS ** -0.5, 0.0)
    outs = _mem_attn_heads(h_ref[:, O_XQ:O_XQ + XA_W], h_ref[:, O_GX:O_GX + XA_W], mk_ref, mv_ref, mqn_ref[...])
    for h, o in enumerate(outs):
        yx_ref[:, h * HEAD_DIM:(h + 1) * HEAD_DIM] = o


def _odd_pre(h3, tabs, qn, kn, mk, mv, mqn, tq):
    b, t, _ = h3.shape
    n_mem = mk.shape[1]
    row_blk = lambda w: pl.BlockSpec((None, tq, w), lambda bi, ti: (bi, ti, 0))
    tab = pl.BlockSpec((tq, LANES), lambda bi, ti: (ti, 0))
    vec = pl.BlockSpec((1, HEAD_DIM), lambda bi, ti: (0, 0))
    mem = pl.BlockSpec((None, n_mem, XA_W), lambda bi, ti: (bi, 0, 0))
    sds = lambda w, dt: jax.ShapeDtypeStruct((b, t, w), dt)
    return pl.pallas_call(
        _odd_pre_kernel,
        grid=(b, t // tq),
        in_specs=[row_blk(O_MAIN),
                  pl.BlockSpec((None, tq, LANES), lambda bi, ti: (bi, ti, O_MAIN // LANES)),
                  tab, tab, tab, tab, tab, tab, vec, vec, mem, mem, vec],
        out_specs=[row_blk(AT_W), row_blk(KV_W), row_blk(KV_W), row_blk(KV_W), row_blk(KV_W),
                   row_blk(IDX_W), row_blk(IDX_DIM), row_blk(IDX_DIM), row_blk(LANES), row_blk(XA_W)],
        out_shape=[sds(AT_W, BF16), sds(KV_W, F32), sds(KV_W, F32), sds(KV_W, BF16), sds(KV_W, BF16),
                   sds(IDX_W, BF16), sds(IDX_DIM, F32), sds(IDX_DIM, BF16), sds(LANES, F32), sds(XA_W, F32)],
        compiler_params=_cparams(("parallel", "parallel")),
        name="odd_pre",
    )(h3, h3, *tabs, qn, kn, mk, mv, mqn)


def _rope_tables(pos):
    pos = pos.astype(F32)[:, None]

    def build(width, reps):
        rd = width // ROPE_FRAC
        half = rd // 2
        inv = ROPE_THETA ** (-jnp.arange(half, dtype=F32) / half)
        ang = pos * inv[None, :]
        cos, sin = jnp.cos(ang), jnp.sin(ang)
        ones = jnp.ones((pos.shape[0], width - rd), F32)
        zeros_h = jnp.zeros((pos.shape[0], half), F32)
        zeros_r = jnp.zeros((pos.shape[0], width - rd), F32)
        c = jnp.concatenate([cos, cos, ones], axis=1)
        s_lo = jnp.concatenate([-sin, zeros_h, zeros_r], axis=1)
        s_hi = jnp.concatenate([zeros_h, sin, zeros_r], axis=1)
        return [jnp.tile(a, (1, reps)) for a in (c, s_lo, s_hi)]

    return build(HEAD_DIM, 1) + build(IDX_DIM, LANES // IDX_DIM)


def _score_keys(acc):
    bits = pltpu.bitcast(acc, I32)
    return bits ^ ((bits >> 31) & 0x7FFFFFFF)


def _topk_threshold(load_keys, n_chunks, rows, k, pos_bits):
    shape = (rows, LANES)
    lane = lax.broadcasted_iota(I32, shape, 1)

    def lane_sum(a):
        return jnp.broadcast_to(jnp.sum(a, axis=1, keepdims=True), shape)

    def bit_step(i, thr):
        cand = thr + lax.shift_left(jnp.int32(1), 31 - i)

        def body(c, acc):
            return acc + jnp.where(load_keys(c) >= cand, 1.0, 0.0)

        cnt = lane_sum(lax.fori_loop(0, n_chunks, body, jnp.zeros(shape, F32)))
        return jnp.where(cnt >= k, cand, thr)

    thr = lax.fori_loop(0, 32, bit_step, jnp.full(shape, INT_MIN, I32))

    def count_body(c, carry):
        gt, eq = carry
        kc = load_keys(c)
        return gt + jnp.where(kc > thr, 1.0, 0.0), eq + jnp.where(kc == thr, 1.0, 0.0)

    gt, eq = lax.fori_loop(0, n_chunks, count_body, (jnp.zeros(shape, F32), jnp.zeros(shape, F32)))
    need = k - lane_sum(gt)
    excess = lane_sum(eq) - need

    def pos_step(i, cut):
        cand = cut + lax.shift_left(jnp.int32(1), pos_bits - 1 - i)

        def body(c, acc):
            hit = (load_keys(c) == thr) & (c * LANES + lane < cand)
            return acc + jnp.where(hit, 1.0, 0.0)

        cnt = lane_sum(lax.fori_loop(0, n_chunks, body, jnp.zeros(shape, F32)))
        return jnp.where(cnt < need, cand, cut)

    any_excess = jnp.max(excess) > 0.0
    cut = lax.cond(any_excess,
                   lambda: lax.fori_loop(0, pos_bits, pos_step, jnp.zeros(shape, I32)),
                   lambda: jnp.full(shape, 2 ** pos_bits, I32))
    return thr, cut


def _selected(keys, thr, cut, pos):
    return ((keys > thr) | ((keys == thr) & (pos <= cut))) & (keys != INT_MIN)


def _dsa_prompt_kernel(q_ref, qi_ref, wi_ref, gc_ref, k_ref, v_ref, ki_ref, o_ref,
                       keys_ref, wib_ref, m_ref, l_ref, acc_ref, *, topk, pos_bits):
    qb = pl.program_id(1)
    n_chunks = qb + 1
    nq = q_ref.shape[0]
    shape = (nq, LANES)
    row = lax.broadcasted_iota(I32, shape, 0)
    lane = lax.broadcasted_iota(I32, shape, 1)
    qpos = qb * nq + row

    for h in range(IDX_HEADS):
        wib_ref[h] = jnp.broadcast_to(wi_ref[:, IDX_DIM + h:IDX_DIM + h + 1], shape) * IDX_DIM ** -0.5

    def score_body(c, carry):
        off = pl.multiple_of(c * LANES, LANES)
        kic = ki_ref[pl.ds(off, LANES), :]
        acc = jnp.zeros(shape, F32)
        for h in range(IDX_HEADS):
            sc = _dot_nt(qi_ref[:, h * IDX_DIM:(h + 1) * IDX_DIM], kic)
            acc = acc + jnp.maximum(sc, 0.0) * wib_ref[h]
        keys_ref[:, pl.ds(off, LANES)] = jnp.where(off + lane <= qpos, _score_keys(acc), INT_MIN)
        return carry

    lax.fori_loop(0, n_chunks, score_body, 0)

    def load_keys(c):
        return keys_ref[:, pl.ds(pl.multiple_of(c * LANES, LANES), LANES)]

    thr, cut = _topk_threshold(load_keys, n_chunks, nq, float(topk), pos_bits)

    m_ref[...] = jnp.full(m_ref.shape, NEG, F32)
    l_ref[...] = jnp.zeros(l_ref.shape, F32)
    acc_ref[...] = jnp.zeros(acc_ref.shape, F32)

    def attn_body(c, carry):
        off = pl.multiple_of(c * LANES, LANES)
        sel = _selected(load_keys(c), thr, cut, off + lane)
        kc = k_ref[pl.ds(off, LANES), :]
        vc = v_ref[pl.ds(off, LANES), :]
        for h in range(AT_HEADS):
            n = h // AT_GROUP
            s = _dot_nt(q_ref[:, h * HEAD_DIM:(h + 1) * HEAD_DIM], kc[:, n * HEAD_DIM:(n + 1) * HEAD_DIM])
            s = jnp.where(sel, s * HEAD_DIM ** -0.5, NEG)
            m_old = m_ref[h]
            m_new = jnp.maximum(m_old, jnp.broadcast_to(jnp.max(s, axis=1, keepdims=True), shape))
            p = jnp.where(sel, jnp.exp(s - m_new), 0.0)
            alpha = jnp.exp(m_old - m_new)
            l_ref[h] = alpha * l_ref[h] + jnp.broadcast_to(jnp.sum(p, axis=1, keepdims=True), shape)
            acc_ref[h] = alpha * acc_ref[h] + _dot(p.astype(BF16), vc[:, n * HEAD_DIM:(n + 1) * HEAD_DIM])
            m_ref[h] = m_new
        return carry

    lax.fori_loop(0, n_chunks, attn_body, 0)
    for h in range(AT_HEADS):
        sl = slice(h * HEAD_DIM, (h + 1) * HEAD_DIM)
        o_ref[:, sl] = acc_ref[h] / l_ref[h] * _silu(gc_ref[:, sl])


def _dsa_prompt(q_bf, qi_bf, wi, h3, k_bf, v_bf, ki_bf, topk):
    b, s, _ = q_bf.shape
    nq = LANES
    pos_bits = max(1, int(math.ceil(math.log2(s))))
    qblk = lambda w: pl.BlockSpec((None, nq, w), lambda bi, qi: (bi, qi, 0))
    full = lambda w: pl.BlockSpec((None, s, w), lambda bi, qi: (bi, 0, 0))
    return pl.pallas_call(
        functools.partial(_dsa_prompt_kernel, topk=topk, pos_bits=pos_bits),
        grid=(b, s // nq),
        in_specs=[qblk(AT_W), qblk(IDX_W), qblk(LANES),
                  pl.BlockSpec((None, nq, AT_W), lambda bi, qi: (bi, qi, O_GC // AT_W)),
                  full(KV_W), full(KV_W), full(IDX_DIM)],
        out_specs=qblk(AT_W),
        out_shape=jax.ShapeDtypeStruct((b, s, AT_W), F32),
        scratch_shapes=[pltpu.VMEM((nq, s), I32),
                        pltpu.VMEM((IDX_HEADS, nq, LANES), F32),
                        pltpu.VMEM((AT_HEADS, nq, LANES), F32),
                        pltpu.VMEM((AT_HEADS, nq, LANES), F32),
                        pltpu.VMEM((AT_HEADS, nq, HEAD_DIM), F32)],
        compiler_params=_cparams(("parallel", "arbitrary")),
        name="dsa_prompt",
    )(q_bf, qi_bf, wi, h3, k_bf, v_bf, ki_bf)


def _dsa_sample_select_kernel(pt_ref, qi_ref, wi_ref, kin_ref, *rest, n_pp, n_steps, topk, pos_bits, t_pad):
    page_refs = rest[:n_pp]
    keys_ref, thr_ref, cut_ref = rest[n_pp:n_pp + 3]
    step = pl.program_id(1)
    qi = qi_ref[...]
    wi = wi_ref[...]

    def page_scores(ki_bf):
        sc = jnp.maximum(_dot_nt(qi, ki_bf), 0.0) * wi
        acc = jnp.zeros((t_pad, LANES), F32)
        for h in range(IDX_HEADS):
            acc = acc + sc[h * t_pad:(h + 1) * t_pad, :]
        return acc

    for j in range(n_pp):
        off = pl.multiple_of((step * n_pp + j) * PAGE_SIZE, PAGE_SIZE)
        keys_ref[:, pl.ds(off, PAGE_SIZE)] = _score_keys(page_scores(page_refs[j][...].astype(BF16)))

    @pl.when(step == n_steps - 1)
    def _():
        past = n_steps * n_pp * PAGE_SIZE
        shape = (t_pad, LANES)
        tok = lax.broadcasted_iota(I32, shape, 0)
        lane = lax.broadcasted_iota(I32, shape, 1)
        new_keys = jnp.where(lane <= tok, _score_keys(page_scores(kin_ref[...])), INT_MIN)
        keys_ref[:, past:past + LANES] = new_keys

        def load_keys(c):
            return keys_ref[:, pl.ds(pl.multiple_of(c * LANES, LANES), LANES)]

        thr, cut = _topk_threshold(load_keys, past // LANES + 1, t_pad, float(topk), pos_bits)
        thr_ref[...] = thr
        cut_ref[...] = cut


def _dsa_sample_attn_kernel(pt_ref, q_ref, keys_ref, keysn_ref, thr_ref, cut_ref, kn_ref, vn_ref, gc_ref, *rest,
                            n_pp, n_steps, t_pad):
    k_refs = rest[:n_pp]
    v_refs = rest[n_pp:2 * n_pp]
    o_ref, m_ref, l_ref, acc_ref = rest[2 * n_pp:2 * n_pp + 4]
    step = pl.program_id(1)
    rows = AT_GROUP * t_pad
    shape = (rows, LANES)
    lane = lax.broadcasted_iota(I32, (t_pad, LANES), 1)

    @pl.when(step == 0)
    def _():
        m_ref[...] = jnp.full(m_ref.shape, NEG, F32)
        l_ref[...] = jnp.zeros(l_ref.shape, F32)
        acc_ref[...] = jnp.zeros(acc_ref.shape, F32)

    thr = thr_ref[...]
    cut = cut_ref[...]

    def attend(keys, pos0, k_page, v_page):
        sel_t = _selected(keys, thr, cut, pos0 + lane)
        sel = jnp.concatenate([sel_t.astype(F32)] * AT_GROUP, axis=0) > 0.0
        for n in range(AT_KV_HEADS):
            sl = slice(n * HEAD_DIM, (n + 1) * HEAD_DIM)
            s = _dot_nt(q_ref[n], k_page[:, sl].astype(BF16))
            s = jnp.where(sel, s * HEAD_DIM ** -0.5, NEG)
            m_old = m_ref[n]
            m_new = jnp.maximum(m_old, jnp.broadcast_to(jnp.max(s, axis=1, keepdims=True), shape))
            p = jnp.where(sel, jnp.exp(s - m_new), 0.0)
            alpha = jnp.exp(m_old - m_new)
            l_ref[n] = alpha * l_ref[n] + jnp.broadcast_to(jnp.sum(p, axis=1, keepdims=True), shape)
            acc_ref[n] = alpha * acc_ref[n] + _dot(p.astype(BF16), v_page[:, sl].astype(BF16))
            m_ref[n] = m_new

    for j in range(n_pp):
        pos0 = (step * n_pp + j) * PAGE_SIZE
        attend(keys_ref[:, j * PAGE_SIZE:(j + 1) * PAGE_SIZE], pos0, k_refs[j][...], v_refs[j][...])

    @pl.when(step == n_steps - 1)
    def _():
        attend(keysn_ref[...], n_steps * n_pp * PAGE_SIZE, kn_ref[...], vn_ref[...])
        for n in range(AT_KV_HEADS):
            o_ref[n] = acc_ref[n] / l_ref[n] * _silu(gc_ref[n])


def _dsa_sample(q_st, qi_st, wi_st, gc_st, k_new, v_new, ki_new, cache_k, cache_v, cache_ik, layer, page_table,
                topk, t_pad):
    bd = q_st.shape[0]
    n_pages = page_table.shape[1]
    n_pp = math.gcd(n_pages, PAGES_PER_STEP)
    n_steps = n_pages // n_pp
    past = n_pages * PAGE_SIZE
    pos_bits = max(1, int(math.ceil(math.log2(past + LANES))))
    n_pool = cache_k.shape[1]
    ck = cache_k.reshape(cache_k.shape[0], n_pool, PAGE_SIZE, KV_W)
    cv = cache_v.reshape(cache_v.shape[0], n_pool, PAGE_SIZE, KV_W)
    rows_i = IDX_HEADS * t_pad
    rows_a = AT_GROUP * t_pad

    def page_spec(width, j):
        return pl.BlockSpec((None, None, PAGE_SIZE, width),
                            lambda bi, si, pt: (layer, pt[bi, si * n_pp + j], 0, 0))

    per_b3 = lambda r, w: pl.BlockSpec((None, r, w), lambda bi, si, pt: (bi, 0, 0))
    keys, thr, cut = pl.pallas_call(
        functools.partial(_dsa_sample_select_kernel, n_pp=n_pp, n_steps=n_steps, topk=topk, pos_bits=pos_bits,
                          t_pad=t_pad),
        grid_spec=pltpu.PrefetchScalarGridSpec(
            num_scalar_prefetch=1,
            grid=(bd, n_steps),
            in_specs=[per_b3(rows_i, IDX_DIM), per_b3(rows_i, LANES), per_b3(PAGE_SIZE, IDX_DIM)]
                     + [page_spec(IDX_DIM, j) for j in range(n_pp)],
            out_specs=[per_b3(t_pad, past + LANES), per_b3(t_pad, LANES), per_b3(t_pad, LANES)]),
        out_shape=[jax.ShapeDtypeStruct((bd, t_pad, past + LANES), I32),
                   jax.ShapeDtypeStruct((bd, t_pad, LANES), I32),
                   jax.ShapeDtypeStruct((bd, t_pad, LANES), I32)],
        compiler_params=_cparams(("parallel", "arbitrary")),
        name="dsa_sample_select",
    )(page_table, qi_st, wi_st, ki_new, *([cache_ik] * n_pp))

    per_b4 = pl.BlockSpec((None, AT_KV_HEADS, rows_a, HEAD_DIM), lambda bi, si, pt: (bi, 0, 0, 0))
    return pl.pallas_call(
        functools.partial(_dsa_sample_attn_kernel, n_pp=n_pp, n_steps=n_steps, t_pad=t_pad),
        grid_spec=pltpu.PrefetchScalarGridSpec(
            num_scalar_prefetch=1,
            grid=(bd, n_steps),
            in_specs=[per_b4,
                      pl.BlockSpec((None, t_pad, n_pp * PAGE_SIZE), lambda bi, si, pt: (bi, 0, si)),
                      pl.BlockSpec((None, t_pad, LANES), lambda bi, si, pt: (bi, 0, past // LANES)),
                      per_b3(t_pad, LANES), per_b3(t_pad, LANES),
                      per_b3(PAGE_SIZE, KV_W), per_b3(PAGE_SIZE, KV_W), per_b4]
                     + [page_spec(KV_W, j) for j in range(n_pp)]
                     + [page_spec(KV_W, j) for j in range(n_pp)],
            out_specs=per_b4,
            scratch_shapes=[pltpu.VMEM((AT_KV_HEADS, rows_a, LANES), F32),
                            pltpu.VMEM((AT_KV_HEADS, rows_a, LANES), F32),
                            pltpu.VMEM((AT_KV_HEADS, rows_a, HEAD_DIM), F32)]),
        out_shape=jax.ShapeDtypeStruct((bd, AT_KV_HEADS, rows_a, HEAD_DIM), F32),
        compiler_params=_cparams(("parallel", "arbitrary")),
        name="dsa_sample_attn",
    )(page_table, q_st, keys, keys, thr, cut, k_new, v_new, gc_st, *([ck] * n_pp), *([cv] * n_pp))


def _row_tile(m):
    return math.gcd(m, 256)


def _even_layer(x3, mk, mv, c0, n0, m0, prm, t_valid):
    norm_g, w_in_bf, bif, hg, lng, lnb, ws, bs, w_out_bf, mqn = prm
    b, t, d = x3.shape
    L = math.gcd(t, LANES)
    x2 = x3.reshape(b * t, d)
    tm = _row_tile(b * t)
    h3 = _in_proj(x2, norm_g, w_in_bf, tm).reshape(b, t, E_COLS)
    m0b = jnp.broadcast_to(m0[:, :, None], (b, ML_HEADS, LANES))
    bs_t = jnp.pad(bs[:, :L].T, ((0, 0), (0, LANES - GM_GROUPS)))
    y, vbn, c, n, mb = _even_mix(h3, c0, n0, m0b, bif, hg.reshape(1, ML_W), lng.reshape(1, GM_W),
                                 lnb.reshape(1, GM_W), ws[:, :L, :L], bs_t, mk, mv, mqn.reshape(1, HEAD_DIM),
                                 L, t_valid)
    x_new = _out_proj(x2, [y.reshape(b * t, -1)], [w_out_bf], tm).reshape(b, t, d)
    return x_new, (c, n, mb[:, :, 0]), vbn


def _odd_common(x3, mk, mv, prm, tabs):
    norm_g, w_in_bf, qn, kn, w_out_c, w_out_x, mqn = prm
    b, t, d = x3.shape
    x2 = x3.reshape(b * t, d)
    tm = _row_tile(b * t)
    h3 = _in_proj(x2, norm_g, w_in_bf, tm).reshape(b, t, O_COLS)
    tq = math.gcd(t, 256)
    pre = _odd_pre(h3, tabs, qn.reshape(1, HEAD_DIM), kn.reshape(1, HEAD_DIM), mk, mv, mqn.reshape(1, HEAD_DIM), tq)
    return x2, tm, h3, pre


def _finish_odd(x2, tm, yc2, yx, prm, shape):
    w_out_c, w_out_x = prm[4], prm[5]
    return _out_proj(x2, [yc2, yx.reshape(x2.shape[0], XA_W)], [w_out_c, w_out_x], tm).reshape(shape)


def kernel(x_prompt, x_sample, state_mlstm_C, state_mlstm_n, state_mlstm_m, cache_attn_k, cache_attn_v,
           cache_idx_k, cache_mem_k, cache_mem_v, page_table, mem_prompt, norm_even, w_in_even, b_if_even,
           mlstm_hnorm, gmlp_ln_g, gmlp_ln_b, gmlp_w, gmlp_b, w_out_even, norm_odd, w_in_odd, attn_qn, attn_kn,
           w_out_odd, w_mem_kv, mem_qn, mem_kn):
    B, S, D = x_prompt.shape
    Bd, T, _ = x_sample.shape
    depth = w_mem_kv.shape[0]
    n_mem = mem_prompt.shape[1]
    past = page_table.shape[1] * PAGE_SIZE
    assert S % LANES == 0 and T <= SAMPLE_PAD

    we = w_in_even
    we = jnp.concatenate([we[..., :2048], we[..., 2056:], we[..., 2048:2056],
                          jnp.zeros(we.shape[:2] + (LANES - 2 * ML_HEADS,), we.dtype)], axis=-1).astype(BF16)
    wo = w_in_odd
    wo = jnp.concatenate([wo[..., 0:1024], wo[..., 1536:2560], wo[..., 1024:1536], wo[..., 2560:3072],
                          wo[..., 3144:4168], wo[..., 3080:3144], wo[..., 3072:3080],
                          jnp.zeros(wo.shape[:2] + (LANES - IDX_DIM - IDX_HEADS,), wo.dtype)], axis=-1).astype(BF16)
    w_out_even_bf = w_out_even.astype(BF16)
    w_out_odd_bf = w_out_odd.astype(BF16)
    bif_pad = jnp.pad(b_if_even, ((0, 0), (0, LANES - 2 * ML_HEADS)))[:, None, :]

    mem_k_p, mem_v_p = _mem_kv(mem_prompt.reshape(B * n_mem, D), w_mem_kv.astype(BF16), mem_kn,
                               _row_tile(B * n_mem))
    mem_k_p = mem_k_p.reshape(depth, B, n_mem, XA_W)
    mem_v_p = mem_v_p.reshape(depth, B, n_mem, XA_W)
    cmk = cache_mem_k.reshape(depth, Bd, n_mem, XA_W)
    cmv = cache_mem_v.reshape(depth, Bd, n_mem, XA_W)

    tp = SAMPLE_PAD
    xp = x_prompt
    xs = jnp.pad(x_sample, ((0, 0), (0, tp - T), (0, 0)))
    tabs_p = _rope_tables(jnp.arange(S))
    tabs_s = _rope_tables(past + jnp.arange(tp))
    zero_c = jnp.zeros((B, ML_HEADS, HEAD_DIM, HEAD_DIM), F32)
    zero_n = jnp.zeros((B, ML_HEADS, HEAD_DIM), F32)
    zero_m = jnp.zeros((B, ML_HEADS), F32)
    topk_p = min(TOPK_MAX, S // 4)
    topk_s = min(TOPK_MAX, (past + T) // 4)

    mlC_p, mln_p, mlm_p, mlC_s, mln_s, mlm_s, gv_s = [], [], [], [], [], [], []
    ak_p, av_p, ik_p, ak_s, av_s, ik_s = [], [], [], [], [], []
    for l in range(depth):
        if l % 2 == 0:
            e = l // 2
            prm = (norm_even[e], we[e], bif_pad[e], mlstm_hnorm[e], gmlp_ln_g[e], gmlp_ln_b[e], gmlp_w[e], gmlp_b[e],
                   w_out_even_bf[e], mem_qn[l])
            xp, (c, n, m), _ = _even_layer(xp, mem_k_p[l], mem_v_p[l], zero_c, zero_n, zero_m, prm, LANES)
            mlC_p.append(c); mln_p.append(n); mlm_p.append(m)
            xs, (c, n, m), vrows = _even_layer(xs, cmk[l], cmv[l], state_mlstm_C[e], state_mlstm_n[e],
                                               state_mlstm_m[e], prm, T)
            mlC_s.append(c); mln_s.append(n); mlm_s.append(m)
            gv_s.append(vrows[:, :T])
        else:
            o = l // 2
            prm = (norm_odd[o], wo[o], attn_qn[o], attn_kn[o], w_out_odd_bf[o, :AT_W], w_out_odd_bf[o, AT_W:],
                   mem_qn[l])
            x2, tm, h3, pre = _odd_common(xp, mem_k_p[l], mem_v_p[l], prm, tabs_p)
            q_bf, k, v, k_bf, v_bf, qi_bf, ki, ki_bf, wi, yx = pre
            yc = _dsa_prompt(q_bf, qi_bf, wi, h3, k_bf, v_bf, ki_bf, topk_p)
            xp = _finish_odd(x2, tm, yc.reshape(B * S, AT_W), yx, prm, xp.shape)
            ak_p.append(k.reshape(B, S, AT_KV_HEADS, HEAD_DIM))
            av_p.append(v.reshape(B, S, AT_KV_HEADS, HEAD_DIM))
            ik_p.append(ki)
            x2, tm, h3, pre = _odd_common(xs, cmk[l], cmv[l], prm, tabs_s)
            q_bf, k, v, k_bf, v_bf, qi_bf, ki, ki_bf, wi, yx = pre
            q_st = q_bf.reshape(Bd, tp, AT_KV_HEADS, AT_GROUP, HEAD_DIM).transpose(0, 2, 3, 1, 4).reshape(
                Bd, AT_KV_HEADS, AT_GROUP * tp, HEAD_DIM)
            gc_st = h3[:, :, O_GC:O_GC + AT_W].reshape(Bd, tp, AT_KV_HEADS, AT_GROUP, HEAD_DIM).transpose(
                0, 2, 3, 1, 4).reshape(Bd, AT_KV_HEADS, AT_GROUP * tp, HEAD_DIM)
            qi_st = qi_bf.reshape(Bd, tp, IDX_HEADS, IDX_DIM).transpose(0, 2, 1, 3).reshape(
                Bd, IDX_HEADS * tp, IDX_DIM)
            wi_h = wi[:, :, IDX_DIM:IDX_DIM + IDX_HEADS] * IDX_DIM ** -0.5
            wi_st = jnp.broadcast_to(wi_h.transpose(0, 2, 1).reshape(Bd, IDX_HEADS * tp, 1),
                                     (Bd, IDX_HEADS * tp, LANES))
            pad_rows = lambda a: jnp.pad(a, ((0, 0), (0, PAGE_SIZE - tp), (0, 0)))
            y_st = _dsa_sample(q_st, qi_st, wi_st, gc_st, pad_rows(k), pad_rows(v), pad_rows(ki_bf),
                               cache_attn_k, cache_attn_v, cache_idx_k, o, page_table, topk_s, tp)
            yc = y_st.reshape(Bd, AT_KV_HEADS, AT_GROUP, tp, HEAD_DIM).transpose(0, 3, 1, 2, 4).reshape(
                Bd * tp, AT_W)
            xs = _finish_odd(x2, tm, yc, yx, prm, xs.shape)
            ak_s.append(k[:, :T].reshape(Bd, T, AT_KV_HEADS, HEAD_DIM))
            av_s.append(v[:, :T].reshape(Bd, T, AT_KV_HEADS, HEAD_DIM))
            ik_s.append(ki[:, :T])
    return (xp, xs[:, :T],
            jnp.stack(mlC_p), jnp.stack(mln_p), jnp.stack(mlm_p),
            jnp.stack(mlC_s), jnp.stack(mln_s), jnp.stack(mlm_s),
            jnp.stack(gv_s),
            jnp.stack(ak_p), jnp.stack(av_p), jnp.stack(ik_p),
            jnp.stack(ak_s), jnp.stack(av_s), jnp.stack(ik_s),
            mem_k_p.reshape(depth, B, n_mem, XA_HEADS, HEAD_DIM),
            mem_v_p.reshape(depth, B, n_mem, XA_HEADS, HEAD_DIM))
```

```python
import functools
import math

import numpy as np
import jax
import jax.numpy as jnp
from jax import lax
from jax.experimental import pallas as pl
from jax.experimental.pallas import tpu as pltpu

F32 = jnp.float32
BF16 = jnp.bfloat16
I32 = jnp.int32

HEAD_DIM = 128
ML_HEADS = 4
GM_GROUPS = 4
AT_HEADS = 8
AT_KV_HEADS = 2
AT_GROUP = AT_HEADS // AT_KV_HEADS
IDX_HEADS = 8
IDX_DIM = 64
XA_HEADS = 4
TOPK_MAX = 256
PAGE_SIZE = 128
ROPE_THETA = 500000.0
ROPE_FRAC = 4
NORM_EPS = 1e-6

ML_W = ML_HEADS * HEAD_DIM
GM_W = GM_GROUPS * HEAD_DIM
XA_W = XA_HEADS * HEAD_DIM
AT_W = AT_HEADS * HEAD_DIM
KV_W = AT_KV_HEADS * HEAD_DIM
IDX_W = IDX_HEADS * IDX_DIM

LANES = 128
SUBLANES = 8
VMEM_LIMIT = 56 * 1024 * 1024
NEG = -1e30
INT_MIN = -2 ** 31
SAMPLE_PAD = SUBLANES
BF16_SUBLANES = 16
VT_ROWS = HEAD_DIM + BF16_SUBLANES
Q_SCALE = HEAD_DIM ** -0.5 * math.log2(math.e)
PAGES_PER_STEP = 8

E_Q, E_K, E_V, E_O, E_GA, E_U, E_VB, E_GB, E_XQ, E_GX, E_IF = (
    0, 512, 1024, 1536, 2048, 2560, 3072, 3584, 4096, 4608, 5120)
E_MAIN = 5120
E_COLS = E_MAIN + LANES
O_Q, O_GC, O_K, O_V, O_QI, O_XQ, O_GX, O_KI = 0, 1024, 2048, 2304, 2560, 3072, 3584, 4096
O_MAIN = 4096
O_COLS = O_MAIN + LANES


def _cparams(sem):
    return pltpu.CompilerParams(dimension_semantics=sem, vmem_limit_bytes=VMEM_LIMIT)


def _sigmoid(x):
    return 1.0 / (1.0 + jnp.exp(-x))


def _silu(x):
    return x * _sigmoid(x)


def _rms_rows(x, g):
    return x * lax.rsqrt(jnp.mean(x * x, axis=-1, keepdims=True) + NORM_EPS) * g


def _dot_nt(a, b):
    return lax.dot_general(a, b, (((1,), (1,)), ((), ())), preferred_element_type=F32)


def _dot_tn(a, b):
    return lax.dot_general(a, b, (((0,), (0,)), ((), ())), preferred_element_type=F32)


def _dot(a, b):
    return jnp.dot(a, b, preferred_element_type=F32)


def _in_proj_kernel(x_ref, g_ref, w_ref, o_ref, *, col_chunk):
    xn = _rms_rows(x_ref[...], g_ref[...]).astype(BF16)
    n = o_ref.shape[-1]
    for c0 in range(0, n, col_chunk):
        c1 = min(c0 + col_chunk, n)
        o_ref[:, c0:c1] = _dot(xn, w_ref[:, c0:c1])


def _in_proj(x2, g, w_bf, tm):
    m, d = x2.shape
    n = w_bf.shape[1]
    return pl.pallas_call(
        functools.partial(_in_proj_kernel, col_chunk=512),
        grid=(m // tm,),
        in_specs=[pl.BlockSpec((tm, d), lambda i: (i, 0)),
                  pl.BlockSpec((1, d), lambda i: (0, 0)),
                  pl.BlockSpec((d, n), lambda i: (0, 0))],
        out_specs=pl.BlockSpec((tm, n), lambda i: (i, 0)),
        out_shape=jax.ShapeDtypeStruct((m, n), F32),
        compiler_params=_cparams(("parallel",)),
        name="in_proj",
    )(x2, g.reshape(1, d), w_bf)


def _out_proj_kernel(*refs, n_in):
    x_ref = refs[0]
    y_refs = refs[1:1 + n_in]
    w_refs = refs[1 + n_in:1 + 2 * n_in]
    o_ref = refs[1 + 2 * n_in]
    acc = x_ref[...]
    for y_ref, w_ref in zip(y_refs, w_refs):
        acc = acc + _dot(y_ref[...].astype(BF16), w_ref[...])
    o_ref[...] = acc


def _out_proj(x2, ys, ws, tm):
    m, d = x2.shape
    n_in = len(ys)
    in_specs = [pl.BlockSpec((tm, d), lambda i: (i, 0))]
    in_specs += [pl.BlockSpec((tm, y.shape[1]), lambda i: (i, 0)) for y in ys]
    in_specs += [pl.BlockSpec(w.shape, lambda i: (0, 0)) for w in ws]
    return pl.pallas_call(
        functools.partial(_out_proj_kernel, n_in=n_in),
        grid=(m // tm,),
        in_specs=in_specs,
        out_specs=pl.BlockSpec((tm, d), lambda i: (i, 0)),
        out_shape=jax.ShapeDtypeStruct((m, d), F32),
        compiler_params=_cparams(("parallel",)),
        name="out_proj",
    )(x2, *ys, *ws)


def _mem_kv_kernel(mem_ref, w_ref, kn_ref, k_ref, v_ref):
    kv = _dot(mem_ref[...].astype(BF16), w_ref[...])
    kn = kn_ref[...]
    for h in range(XA_HEADS):
        sl = slice(h * HEAD_DIM, (h + 1) * HEAD_DIM)
        k_ref[:, sl] = _rms_rows(kv[:, sl], kn)
    v_ref[...] = kv[:, XA_W:]


def _mem_kv(mem2, w_kv_bf, mem_kn, tm):
    m, d = mem2.shape
    depth = w_kv_bf.shape[0]
    out = jax.ShapeDtypeStruct((depth, m, XA_W), F32)
    return pl.pallas_call(
        _mem_kv_kernel,
        grid=(depth, m // tm),
        in_specs=[pl.BlockSpec((tm, d), lambda l, i: (i, 0)),
                  pl.BlockSpec((None, d, 2 * XA_W), lambda l, i: (l, 0, 0)),
                  pl.BlockSpec((None, 1, HEAD_DIM), lambda l, i: (l, 0, 0))],
        out_specs=[pl.BlockSpec((None, tm, XA_W), lambda l, i: (l, i, 0)),
                   pl.BlockSpec((None, tm, XA_W), lambda l, i: (l, i, 0))],
        out_shape=[out, out],
        compiler_params=_cparams(("parallel", "parallel")),
        name="mem_kv",
    )(mem2, w_kv_bf, mem_kn.reshape(depth, 1, HEAD_DIM))


def _mem_attn_heads(xq, gx, mk_ref, mv_ref, qn):
    outs = []
    for h in range(XA_HEADS):
        sl = slice(h * HEAD_DIM, (h + 1) * HEAD_DIM)
        q = _rms_rows(xq[:, sl], qn).astype(BF16)
        s = _dot_nt(q, mk_ref[:, sl].astype(BF16)) * HEAD_DIM ** -0.5
        s = s - jnp.max(s, axis=-1, keepdims=True)
        p = jnp.exp(s)
        p = p / jnp.sum(p, axis=-1, keepdims=True)
        o = _dot(p.astype(BF16), mv_ref[:, sl].astype(BF16))
        outs.append(o * _silu(gx[:, sl]))
    return outs


def _even_mix_kernel(h_ref, if_ref, c0_ref, n0_ref, m0_ref, bif_ref, hg_ref, lng_ref, lnb_ref,
                     ws_ref, bs_ref, mk_ref, mv_ref, qn_ref,
                     y_ref, vbn_ref, c_ref, n_ref, m_ref, *, t_valid):
    L = h_ref.shape[0]
    ci = pl.program_id(1)

    @pl.when(ci == 0)
    def _():
        c_ref[...] = c0_ref[...]
        n_ref[...] = n0_ref[...]
        m_ref[...] = m0_ref[...]

    row = lax.broadcasted_iota(I32, (L, L), 0)
    col = lax.broadcasted_iota(I32, (L, L), 1)
    eye = row == col
    row_le_col = row <= col
    col_le_row = col <= row
    tcol = lax.broadcasted_iota(I32, (L, 1), 0)

    gates = if_ref[...] + bif_ref[...]
    log_f_all = jnp.minimum(gates, 0.0) - jnp.log(1.0 + jnp.exp(-jnp.abs(gates)))

    for h in range(ML_HEADS):
        sl = slice(h * HEAD_DIM, (h + 1) * HEAD_DIM)
        li_col = gates[:, h:h + 1]
        lf_col = log_f_all[:, ML_HEADS + h:ML_HEADS + h + 1]
        if t_valid < L:
            li_col = jnp.where(tcol < t_valid, li_col, NEG)
            lf_col = jnp.where(tcol < t_valid, lf_col, 0.0)
        li_b = jnp.broadcast_to(li_col, (L, L))
        lf_b = jnp.broadcast_to(lf_col, (L, L))
        li_row = jnp.sum(jnp.where(eye, li_b, 0.0), axis=0, keepdims=True)
        lf_row = jnp.sum(jnp.where(eye, lf_b, 0.0), axis=0, keepdims=True)
        b_row = jnp.sum(jnp.where(row_le_col, lf_b, 0.0), axis=0, keepdims=True)
        b_col = jnp.sum(jnp.where(col_le_row, jnp.broadcast_to(lf_row, (L, L)), 0.0),
                        axis=1, keepdims=True)
        m_prev = m_ref[h:h + 1, 0:1]
        n_prev = n_ref[h:h + 1, :]
        c_prev = c_ref[h]

        dlog = jnp.where(col_le_row, b_col - b_row + li_row, NEG)
        inter = b_col + m_prev
        m_t = jnp.maximum(inter, jnp.max(dlog, axis=1, keepdims=True))
        w_intra = jnp.exp(dlog - m_t)
        w_inter = jnp.exp(inter - m_t)
        q = h_ref[:, E_Q + h * HEAD_DIM:E_Q + (h + 1) * HEAD_DIM]
        k = h_ref[:, E_K + h * HEAD_DIM:E_K + (h + 1) * HEAD_DIM] * HEAD_DIM ** -0.5
        v = h_ref[:, E_V + h * HEAD_DIM:E_V + (h + 1) * HEAD_DIM]
        q_bf = q.astype(BF16)
        k_bf = k.astype(BF16)
        qk = _dot_nt(q_bf, k_bf) * w_intra
        num = _dot(qk.astype(BF16), v.astype(BF16)) + w_inter * _dot(q_bf, c_prev.astype(BF16))
        den = jnp.sum(qk, axis=1, keepdims=True) + w_inter * jnp.sum(q * n_prev, axis=1, keepdims=True)
        hh = num / jnp.maximum(jnp.abs(den), jnp.exp(-m_t))

        b_end = b_col[L - 1:L, :]
        g = b_end - b_col + li_col
        m_new = jnp.maximum(b_end + m_prev, jnp.max(g, axis=0, keepdims=True))
        a_prev = jnp.exp(b_end + m_prev - m_new)
        a_tok = jnp.exp(g - m_new)
        c_ref[h] = a_prev * c_prev + _dot_tn(k_bf, (a_tok * v).astype(BF16))
        n_ref[h:h + 1, :] = a_prev * n_prev + jnp.sum(a_tok * k, axis=0, keepdims=True)
        m_ref[h:h + 1, :] = jnp.broadcast_to(m_new, (1, LANES))

        ha = _rms_rows(hh, hg_ref[:, sl])
        y_ref[:, sl] = ha * _sigmoid(h_ref[:, E_O + h * HEAD_DIM:E_O + (h + 1) * HEAD_DIM]) * _silu(
            h_ref[:, E_GA + h * HEAD_DIM:E_GA + (h + 1) * HEAD_DIM])

    vb = h_ref[:, E_VB:E_VB + GM_W]
    vc = vb - jnp.mean(vb, axis=-1, keepdims=True)
    var = jnp.mean(vc * vc, axis=-1, keepdims=True)
    vbn = vc * lax.rsqrt(var + NORM_EPS) * lng_ref[...] + lnb_ref[...]
    vbn_ref[...] = vbn
    for g_i in range(GM_GROUPS):
        sl = slice(g_i * HEAD_DIM, (g_i + 1) * HEAD_DIM)
        w = jnp.where(col_le_row, ws_ref[g_i], 0.0).astype(BF16)
        mixed = _dot(w, vbn[:, sl].astype(BF16)) + bs_ref[:, g_i:g_i + 1]
        y_ref[:, ML_W + g_i * HEAD_DIM:ML_W + (g_i + 1) * HEAD_DIM] = (
            h_ref[:, E_U + g_i * HEAD_DIM:E_U + (g_i + 1) * HEAD_DIM] * mixed
            * _silu(h_ref[:, E_GB + g_i * HEAD_DIM:E_GB + (g_i + 1) * HEAD_DIM]))

    outs = _mem_attn_heads(h_ref[:, E_XQ:E_XQ + XA_W], h_ref[:, E_GX:E_GX + XA_W], mk_ref, mv_ref, qn_ref[...])
    for h, o in enumerate(outs):
        y_ref[:, ML_W + GM_W + h * HEAD_DIM:ML_W + GM_W + (h + 1) * HEAD_DIM] = o


def _even_mix(h3, c0, n0, m0b, bif, hg, lng, lnb, ws, bs_t, mk, mv, qn, L, t_valid):
    b, t, _ = h3.shape
    nc = t // L
    n_mem = mk.shape[1]
    full2 = lambda bi, ci: (0, 0)
    return pl.pallas_call(
        functools.partial(_even_mix_kernel, t_valid=t_valid),
        grid=(b, nc),
        in_specs=[pl.BlockSpec((None, L, E_MAIN), lambda bi, ci: (bi, ci, 0)),
                  pl.BlockSpec((None, L, LANES), lambda bi, ci: (bi, ci, E_MAIN // LANES)),
                  pl.BlockSpec((None, ML_HEADS, HEAD_DIM, HEAD_DIM), lambda bi, ci: (bi, 0, 0, 0)),
                  pl.BlockSpec((None, ML_HEADS, HEAD_DIM), lambda bi, ci: (bi, 0, 0)),
                  pl.BlockSpec((None, ML_HEADS, LANES), lambda bi, ci: (bi, 0, 0)),
                  pl.BlockSpec((1, LANES), full2),
                  pl.BlockSpec((1, ML_W), full2),
                  pl.BlockSpec((1, GM_W), full2),
                  pl.BlockSpec((1, GM_W), full2),
                  pl.BlockSpec((GM_GROUPS, L, L), lambda bi, ci: (0, 0, 0)),
                  pl.BlockSpec((L, LANES), full2),
                  pl.BlockSpec((None, n_mem, XA_W), lambda bi, ci: (bi, 0, 0)),
                  pl.BlockSpec((None, n_mem, XA_W), lambda bi, ci: (bi, 0, 0)),
                  pl.BlockSpec((1, HEAD_DIM), full2)],
        out_specs=[pl.BlockSpec((None, L, ML_W + GM_W + XA_W), lambda bi, ci: (bi, ci, 0)),
                   pl.BlockSpec((None, L, GM_W), lambda bi, ci: (bi, ci, 0)),
                   pl.BlockSpec((None, ML_HEADS, HEAD_DIM, HEAD_DIM), lambda bi, ci: (bi, 0, 0, 0)),
                   pl.BlockSpec((None, ML_HEADS, HEAD_DIM), lambda bi, ci: (bi, 0, 0)),
                   pl.BlockSpec((None, ML_HEADS, LANES), lambda bi, ci: (bi, 0, 0))],
        out_shape=[jax.ShapeDtypeStruct((b, t, ML_W + GM_W + XA_W), F32),
                   jax.ShapeDtypeStruct((b, t, GM_W), F32),
                   jax.ShapeDtypeStruct((b, ML_HEADS, HEAD_DIM, HEAD_DIM), F32),
                   jax.ShapeDtypeStruct((b, ML_HEADS, HEAD_DIM), F32),
                   jax.ShapeDtypeStruct((b, ML_HEADS, LANES), F32)],
        compiler_params=_cparams(("parallel", "arbitrary")),
        name="even_mix",
    )(h3, h3, c0, n0, m0b, bif, hg, lng, lnb, ws, bs_t, mk, mv, qn)


def _rope_lanes(x, cos, sin_lo, sin_hi, half):
    n = x.shape[-1]
    return x * cos + pltpu.roll(x, n - half, 1) * sin_lo + pltpu.roll(x, half, 1) * sin_hi


def _odd_pre_kernel(h_ref, kiwi_ref, ca_ref, sa1_ref, sa2_ref, ci_ref, si1_ref, si2_ref,
                    qn_ref, kn_ref, mk_ref, mv_ref, mqn_ref,
                    q_ref, k_ref, v_ref, kbf_ref, vbf_ref, qi_ref, ki_ref, kibf_ref, wi_ref, yx_ref, *, v_transposed):
    ca, sa1, sa2 = ca_ref[...], sa1_ref[...], sa2_ref[...]
    ci, si1, si2 = ci_ref[...], si1_ref[...], si2_ref[...]
    a_half = HEAD_DIM // ROPE_FRAC // 2
    i_half = IDX_DIM // ROPE_FRAC // 2
    for h in range(AT_HEADS):
        sl = slice(O_Q + h * HEAD_DIM, O_Q + (h + 1) * HEAD_DIM)
        qh = _rope_lanes(_rms_rows(h_ref[:, sl], qn_ref[...]), ca, sa1, sa2, a_half)
        q_ref[:, h * HEAD_DIM:(h + 1) * HEAD_DIM] = (qh * Q_SCALE).astype(BF16)
    for h in range(AT_KV_HEADS):
        sl = slice(O_K + h * HEAD_DIM, O_K + (h + 1) * HEAD_DIM)
        kh = _rope_lanes(_rms_rows(h_ref[:, sl], kn_ref[...]), ca, sa1, sa2, a_half)
        k_ref[:, h * HEAD_DIM:(h + 1) * HEAD_DIM] = kh
        kbf_ref[:, h * HEAD_DIM:(h + 1) * HEAD_DIM] = kh.astype(BF16)
    v = h_ref[:, O_V:O_V + KV_W]
    v_ref[...] = v
    if v_transposed:
        tail_row = lax.broadcasted_iota(I32, (VT_ROWS - HEAD_DIM, v.shape[0]), 0)
        tail = jnp.where(tail_row == 0, 1.0, 0.0).astype(BF16)
        for n in range(AT_KV_HEADS):
            vbf_ref[n, :HEAD_DIM, :] = v[:, n * HEAD_DIM:(n + 1) * HEAD_DIM].T.astype(BF16)
            vbf_ref[n, HEAD_DIM:, :] = tail
    else:
        vbf_ref[...] = v.astype(BF16)
    for j in range(IDX_W // LANES):
        sl = slice(O_QI + j * LANES, O_QI + (j + 1) * LANES)
        qi_ref[:, j * LANES:(j + 1) * LANES] = _rope_lanes(h_ref[:, sl], ci, si1, si2, i_half).astype(BF16)
    kiwi = kiwi_ref[...]
    ki = _rope_lanes(kiwi, ci, si1, si2, i_half)[:, :IDX_DIM]
    ki_ref[...] = ki
    kibf_ref[...] = ki.astype(BF16)
    lane = lax.broadcasted_iota(I32, kiwi.shape, 1)
    wsel = (lane >= IDX_DIM) & (lane < IDX_DIM + IDX_HEADS)
    wi_ref[...] = jnp.where(wsel, kiwi * IDX_HEADS ** -0.5, 0.0)
    outs = _mem_attn_heads(h_ref[:, O_XQ:O_XQ + XA_W], h_ref[:, O_GX:O_GX + XA_W], mk_ref, mv_ref, mqn_ref[...])
    for h, o in enumerate(outs):
        yx_ref[:, h * HEAD_DIM:(h + 1) * HEAD_DIM] = o


def _odd_pre(h3, tabs, qn, kn, mk, mv, mqn, tq, v_transposed):
    b, t, _ = h3.shape
    n_mem = mk.shape[1]
    row_blk = lambda w: pl.BlockSpec((None, tq, w), lambda bi, ti: (bi, ti, 0))
    tab = pl.BlockSpec((tq, LANES), lambda bi, ti: (ti, 0))
    vec = pl.BlockSpec((1, HEAD_DIM), lambda bi, ti: (0, 0))
    mem = pl.BlockSpec((None, n_mem, XA_W), lambda bi, ti: (bi, 0, 0))
    sds = lambda w, dt: jax.ShapeDtypeStruct((b, t, w), dt)
    if v_transposed:
        vbf_spec = pl.BlockSpec((None, AT_KV_HEADS, VT_ROWS, tq), lambda bi, ti: (bi, 0, 0, ti))
        vbf_shape = jax.ShapeDtypeStruct((b, AT_KV_HEADS, VT_ROWS, t), BF16)
    else:
        vbf_spec, vbf_shape = row_blk(KV_W), sds(KV_W, BF16)
    return pl.pallas_call(
        functools.partial(_odd_pre_kernel, v_transposed=v_transposed),
        grid=(b, t // tq),
        in_specs=[row_blk(O_MAIN),
                  pl.BlockSpec((None, tq, LANES), lambda bi, ti: (bi, ti, O_MAIN // LANES)),
                  tab, tab, tab, tab, tab, tab, vec, vec, mem, mem, vec],
        out_specs=[row_blk(AT_W), row_blk(KV_W), row_blk(KV_W), row_blk(KV_W), vbf_spec,
                   row_blk(IDX_W), row_blk(IDX_DIM), row_blk(IDX_DIM), row_blk(LANES), row_blk(XA_W)],
        out_shape=[sds(AT_W, BF16), sds(KV_W, F32), sds(KV_W, F32), sds(KV_W, BF16), vbf_shape,
                   sds(IDX_W, BF16), sds(IDX_DIM, F32), sds(IDX_DIM, BF16), sds(LANES, F32), sds(XA_W, F32)],
        compiler_params=_cparams(("parallel", "parallel")),
        name="odd_pre",
    )(h3, h3, *tabs, qn, kn, mk, mv, mqn)


def _rope_tables(pos):
    pos = pos.astype(F32)[:, None]

    def build(width, reps):
        rd = width // ROPE_FRAC
        half = rd // 2
        inv = ROPE_THETA ** (-jnp.arange(half, dtype=F32) / half)
        ang = pos * inv[None, :]
        cos, sin = jnp.cos(ang), jnp.sin(ang)
        ones = jnp.ones((pos.shape[0], width - rd), F32)
        zeros_h = jnp.zeros((pos.shape[0], half), F32)
        zeros_r = jnp.zeros((pos.shape[0], width - rd), F32)
        c = jnp.concatenate([cos, cos, ones], axis=1)
        s_lo = jnp.concatenate([-sin, zeros_h, zeros_r], axis=1)
        s_hi = jnp.concatenate([zeros_h, sin, zeros_r], axis=1)
        return [jnp.tile(a, (1, reps)) for a in (c, s_lo, s_hi)]

    return build(HEAD_DIM, 1) + build(IDX_DIM, LANES // IDX_DIM)


def _score_keys(acc):
    bits = pltpu.bitcast(acc, I32)
    return bits ^ ((bits >> 31) & 0x7FFFFFFF)


def _topk_threshold(load_keys, n_chunks, chunk_shape, key_axis, k, pos_bits):
    red_shape = tuple(1 if a == key_axis else d for a, d in enumerate(chunk_shape))
    pos_in_chunk = lax.broadcasted_iota(I32, chunk_shape, key_axis)
    zeros = jnp.zeros(chunk_shape, F32)

    def key_sum(a):
        return jnp.sum(a, axis=key_axis, keepdims=True)

    def bit_step(i, thr):
        cand = thr + lax.shift_left(jnp.int32(1), 31 - i)
        cand_b = jnp.broadcast_to(cand, chunk_shape)

        def body(c, acc):
            return acc + jnp.where(load_keys(c) >= cand_b, 1.0, 0.0)

        cnt = key_sum(lax.fori_loop(0, n_chunks, body, zeros))
        return jnp.where(cnt >= k, cand, thr)

    thr = lax.fori_loop(0, 32, bit_step, jnp.full(red_shape, INT_MIN, I32))
    thr_b = jnp.broadcast_to(thr, chunk_shape)

    def count_body(c, carry):
        gt, eq = carry
        kc = load_keys(c)
        return gt + jnp.where(kc > thr_b, 1.0, 0.0), eq + jnp.where(kc == thr_b, 1.0, 0.0)

    gt, eq = lax.fori_loop(0, n_chunks, count_body, (zeros, zeros))
    need = k - key_sum(gt)
    excess = key_sum(eq) - need

    def pos_step(i, cut):
        cand = cut + lax.shift_left(jnp.int32(1), pos_bits - 1 - i)
        cand_b = jnp.broadcast_to(cand, chunk_shape)

        def body(c, acc):
            hit = (load_keys(c) == thr_b) & (c * chunk_shape[key_axis] + pos_in_chunk < cand_b)
            return acc + jnp.where(hit, 1.0, 0.0)

        cnt = key_sum(lax.fori_loop(0, n_chunks, body, zeros))
        return jnp.where(cnt < need, cand, cut)

    any_excess = jnp.max(excess) > 0.0
    cut = lax.cond(any_excess,
                   lambda: lax.fori_loop(0, pos_bits, pos_step, jnp.zeros(red_shape, I32)),
                   lambda: jnp.full(red_shape, 2 ** pos_bits, I32))
    return thr, cut, any_excess


def _selected(keys, thr, cut, pos):
    return ((keys > thr) | ((keys == thr) & (pos <= cut))) & (keys != INT_MIN)


def _dsa_prompt_kernel(q_ref, qi_ref, wi_ref, gc_ref, k_ref, vt_ref, ki_ref, o_ref,
                       keys_ref, qt_ref, qit_ref, wit_ref, m_ref, l_ref, acc_ref, *, topk, pos_bits, ck, ca):
    qb = pl.program_id(1)
    nq = q_ref.shape[0]
    n_chunks = (qb * nq) // ck + 1
    chunk = (ck, LANES)
    kpos_in = lax.broadcasted_iota(I32, chunk, 0)
    qpos = qb * nq + lax.broadcasted_iota(I32, chunk, 1)

    for h in range(AT_HEADS):
        sl = slice(h * HEAD_DIM, (h + 1) * HEAD_DIM)
        qt_ref[:, sl] = q_ref[:, sl].astype(F32).T.astype(BF16)
    for j in range(IDX_W // LANES):
        t = qi_ref[:, j * LANES:(j + 1) * LANES].astype(F32).T.astype(BF16)
        for i in range(LANES // IDX_DIM):
            h = j * (LANES // IDX_DIM) + i
            qit_ref[:, h * LANES:(h + 1) * LANES] = t[i * IDX_DIM:(i + 1) * IDX_DIM, :]
    wit_ref[...] = wi_ref[...].T[IDX_DIM:IDX_DIM + IDX_HEADS, :] * IDX_DIM ** -0.5

    def score_body(c, carry):
        off = pl.multiple_of(c * ck, ck)
        sc = _dot(ki_ref[pl.ds(off, ck), :], qit_ref[...])
        acc = jnp.zeros(chunk, F32)
        for h in range(IDX_HEADS):
            acc = acc + jnp.maximum(sc[:, h * LANES:(h + 1) * LANES], 0.0) * wit_ref[h:h + 1, :]
        keys_ref[pl.ds(off, ck), :] = jnp.where(off + kpos_in <= qpos, _score_keys(acc), INT_MIN)
        return carry

    lax.fori_loop(0, n_chunks, score_body, 0)

    def load_keys(c):
        return keys_ref[pl.ds(pl.multiple_of(c * ck, ck), ck), :]

    thr, cut, any_excess = _topk_threshold(load_keys, n_chunks, chunk, 0, float(topk), pos_bits)

    @pl.when(any_excess)
    def _():
        def demote_body(c, carry):
            off = pl.multiple_of(c * ck, ck)
            kc = keys_ref[pl.ds(off, ck), :]
            surplus = (kc == thr) & (off + kpos_in > cut) & (thr > INT_MIN)
            keys_ref[pl.ds(off, ck), :] = jnp.where(surplus, INT_MIN, kc)
            return carry

        lax.fori_loop(0, n_chunks, demote_body, 0)

    thr_eff = jnp.broadcast_to(jnp.maximum(thr, INT_MIN + 1), (ca, LANES))

    n_att = (qb * nq) // ca + 1

    @pl.when(n_att * ca > n_chunks * ck)
    def _():
        keys_ref[pl.ds(pl.multiple_of(n_chunks * ck, ck), ck), :] = jnp.full(chunk, INT_MIN, I32)

    m_ref[...] = jnp.full(m_ref.shape, NEG, F32)
    l_ref[...] = jnp.zeros(l_ref.shape, F32)
    acc_ref[...] = jnp.zeros(acc_ref.shape, F32)
    pair = 2 * HEAD_DIM
    n_pairs = AT_HEADS // 2

    def attn_body(c, carry):
        off = pl.multiple_of(c * ca, ca)
        bias = jnp.where(keys_ref[pl.ds(off, ca), :] >= thr_eff, 0.0, NEG)
        s = jnp.concatenate(
            [_dot(k_ref[pl.ds(off, ca), ((2 * hp) // AT_GROUP) * HEAD_DIM:((2 * hp) // AT_GROUP + 1) * HEAD_DIM],
                  qt_ref[:, hp * pair:(hp + 1) * pair]) for hp in range(n_pairs)], axis=1)
        s = s + jnp.concatenate([bias] * AT_HEADS, axis=1)
        m_old = m_ref[...]
        m_new = jnp.maximum(m_old, jnp.max(s, axis=0, keepdims=True))
        p = jnp.exp2(s - m_new).astype(BF16)
        alpha = jnp.exp2(m_old - m_new)
        m_ref[...] = m_new
        for hp in range(n_pairs):
            n = (2 * hp) // AT_GROUP
            sl = slice(hp * pair, (hp + 1) * pair)
            pv = _dot(vt_ref[n, :, pl.ds(off, ca)], p[:, sl])
            acc_ref[:, sl] = alpha[:, sl] * acc_ref[:, sl] + pv[:HEAD_DIM]
            l_ref[:, sl] = alpha[:, sl] * l_ref[:, sl] + pv[HEAD_DIM:HEAD_DIM + 1]
        return carry

    lax.fori_loop(0, n_att, attn_body, 0)
    for h in range(AT_HEADS):
        sl = slice(h * HEAD_DIM, (h + 1) * HEAD_DIM)
        o_ref[:, sl] = (acc_ref[:, sl] / l_ref[:, sl]).T * _silu(gc_ref[:, sl])


def _dsa_prompt(q_bf, qi_bf, wi, h3, k_bf, vt_bf, ki_bf, topk):
    b, s, _ = q_bf.shape
    nq = LANES
    ca = math.gcd(s, 4 * LANES)
    ck = max(ca // 2, LANES)
    assert s % ca == 0 and ca % ck == 0 and ca // ck <= 2
    pos_bits = max(1, int(math.ceil(math.log2(s))))
    qblk = lambda w: pl.BlockSpec((None, nq, w), lambda bi, qi: (bi, qi, 0))
    full = lambda w: pl.BlockSpec((None, s, w), lambda bi, qi: (bi, 0, 0))
    return pl.pallas_call(
        functools.partial(_dsa_prompt_kernel, topk=topk, pos_bits=pos_bits, ck=ck, ca=ca),
        grid=(b, s // nq),
        in_specs=[qblk(AT_W), qblk(IDX_W), qblk(LANES),
                  pl.BlockSpec((None, nq, AT_W), lambda bi, qi: (bi, qi, O_GC // AT_W)),
                  full(KV_W),
                  pl.BlockSpec((None, AT_KV_HEADS, VT_ROWS, s), lambda bi, qi: (bi, 0, 0, 0)),
                  full(IDX_DIM)],
        out_specs=qblk(AT_W),
        out_shape=jax.ShapeDtypeStruct((b, s, AT_W), F32),
        scratch_shapes=[pltpu.VMEM((s, LANES), I32),
                        pltpu.VMEM((HEAD_DIM, AT_HEADS * nq), BF16),
                        pltpu.VMEM((IDX_DIM, IDX_HEADS * nq), BF16),
                        pltpu.VMEM((IDX_HEADS, nq), F32),
                        pltpu.VMEM((1, AT_HEADS * nq), F32),
                        pltpu.VMEM((1, AT_HEADS * nq), F32),
                        pltpu.VMEM((HEAD_DIM, AT_HEADS * nq), F32)],
        compiler_params=_cparams(("parallel", "arbitrary")),
        name="dsa_prompt",
    )(q_bf, qi_bf, wi, h3, k_bf, vt_bf, ki_bf)


def _dsa_sample_select_kernel(pt_ref, qi_ref, wi_ref, kin_ref, *rest, n_pp, n_steps, topk, pos_bits, t_pad):
    page_refs = rest[:n_pp]
    keys_ref, thr_ref, cut_ref = rest[n_pp:n_pp + 3]
    step = pl.program_id(1)
    qi = qi_ref[...]
    wi = wi_ref[...]

    def page_scores(ki_bf):
        sc = jnp.maximum(_dot_nt(qi, ki_bf), 0.0) * wi
        acc = jnp.zeros((t_pad, LANES), F32)
        for h in range(IDX_HEADS):
            acc = acc + sc[h * t_pad:(h + 1) * t_pad, :]
        return acc

    for j in range(n_pp):
        off = pl.multiple_of((step * n_pp + j) * PAGE_SIZE, PAGE_SIZE)
        keys_ref[:, pl.ds(off, PAGE_SIZE)] = _score_keys(page_scores(page_refs[j][...].astype(BF16)))

    @pl.when(step == n_steps - 1)
    def _():
        past = n_steps * n_pp * PAGE_SIZE
        shape = (t_pad, LANES)
        tok = lax.broadcasted_iota(I32, shape, 0)
        lane = lax.broadcasted_iota(I32, shape, 1)
        new_keys = jnp.where(lane <= tok, _score_keys(page_scores(kin_ref[...])), INT_MIN)
        keys_ref[:, past:past + LANES] = new_keys

        def load_keys(c):
            return keys_ref[:, pl.ds(pl.multiple_of(c * LANES, LANES), LANES)]

        thr, cut, _ = _topk_threshold(load_keys, past // LANES + 1, shape, 1, float(topk), pos_bits)
        thr_ref[...] = jnp.broadcast_to(thr, shape)
        cut_ref[...] = jnp.broadcast_to(cut, shape)


def _dsa_sample_attn_kernel(pt_ref, q_ref, keys_ref, keysn_ref, thr_ref, cut_ref, kn_ref, vn_ref, gc_ref, *rest,
                            n_pp, n_steps, t_pad):
    k_refs = rest[:n_pp]
    v_refs = rest[n_pp:2 * n_pp]
    o_ref, m_ref, l_ref, acc_ref = rest[2 * n_pp:2 * n_pp + 4]
    step = pl.program_id(1)
    rows = AT_GROUP * t_pad
    shape = (rows, LANES)
    lane = lax.broadcasted_iota(I32, (t_pad, LANES), 1)

    @pl.when(step == 0)
    def _():
        m_ref[...] = jnp.full(m_ref.shape, NEG, F32)
        l_ref[...] = jnp.zeros(l_ref.shape, F32)
        acc_ref[...] = jnp.zeros(acc_ref.shape, F32)

    thr = thr_ref[...]
    cut = cut_ref[...]

    def attend(blocks):
        biases = []
        for keys, pos0, _, _ in blocks:
            bias_t = jnp.where(_selected(keys, thr, cut, pos0 + lane), 0.0, NEG)
            biases.append(jnp.concatenate([bias_t] * AT_GROUP, axis=0))
        for n in range(AT_KV_HEADS):
            s = jnp.concatenate([_dot_nt(q_ref[n], get_k(n).astype(BF16)) + bias
                                 for (_, _, get_k, _), bias in zip(blocks, biases)], axis=1)
            m_old = m_ref[n]
            m_new = jnp.maximum(m_old, jnp.broadcast_to(jnp.max(s, axis=1, keepdims=True), shape))
            p = jnp.exp2(s - jnp.concatenate([m_new] * len(blocks), axis=1))
            alpha = jnp.exp2(m_old - m_new)
            l_ref[n] = alpha * l_ref[n] + jnp.broadcast_to(jnp.sum(p, axis=1, keepdims=True), shape)
            acc = alpha * acc_ref[n]
            for j, (_, _, _, get_v) in enumerate(blocks):
                acc = acc + _dot(p[:, j * PAGE_SIZE:(j + 1) * PAGE_SIZE].astype(BF16), get_v(n).astype(BF16))
            acc_ref[n] = acc
            m_ref[n] = m_new

    attend([(keys_ref[:, j * PAGE_SIZE:(j + 1) * PAGE_SIZE], (step * n_pp + j) * PAGE_SIZE,
             functools.partial(lambda r, n: r[:, n, :], k_refs[j]),
             functools.partial(lambda r, n: r[:, n, :], v_refs[j])) for j in range(n_pp)])

    @pl.when(step == n_steps - 1)
    def _():
        attend([(keysn_ref[...], n_steps * n_pp * PAGE_SIZE,
                 lambda n: kn_ref[:, n * HEAD_DIM:(n + 1) * HEAD_DIM],
                 lambda n: vn_ref[:, n * HEAD_DIM:(n + 1) * HEAD_DIM])])
        for n in range(AT_KV_HEADS):
            o_ref[n] = acc_ref[n] / l_ref[n] * _silu(gc_ref[n])


def _dsa_sample(q_st, qi_st, wi_st, gc_st, k_new, v_new, ki_new, cache_k, cache_v, cache_ik, layer, page_table,
                topk, t_pad):
    bd = q_st.shape[0]
    n_pages = page_table.shape[1]
    n_pp = math.gcd(n_pages, PAGES_PER_STEP)
    n_steps = n_pages // n_pp
    past = n_pages * PAGE_SIZE
    pos_bits = max(1, int(math.ceil(math.log2(past + LANES))))
    rows_i = IDX_HEADS * t_pad
    rows_a = AT_GROUP * t_pad

    def page_spec(width, j):
        return pl.BlockSpec((None, None, PAGE_SIZE, width),
                            lambda bi, si, pt: (layer, pt[bi, si * n_pp + j], 0, 0))

    def kv_page_spec(j):
        return pl.BlockSpec((None, None, PAGE_SIZE, AT_KV_HEADS, HEAD_DIM),
                            lambda bi, si, pt: (layer, pt[bi, si * n_pp + j], 0, 0, 0))

    per_b3 = lambda r, w: pl.BlockSpec((None, r, w), lambda bi, si, pt: (bi, 0, 0))
    keys, thr, cut = pl.pallas_call(
        functools.partial(_dsa_sample_select_kernel, n_pp=n_pp, n_steps=n_steps, topk=topk, pos_bits=pos_bits,
                          t_pad=t_pad),
        grid_spec=pltpu.PrefetchScalarGridSpec(
            num_scalar_prefetch=1,
            grid=(bd, n_steps),
            in_specs=[per_b3(rows_i, IDX_DIM), per_b3(rows_i, LANES), per_b3(PAGE_SIZE, IDX_DIM)]
                     + [page_spec(IDX_DIM, j) for j in range(n_pp)],
            out_specs=[per_b3(t_pad, past + LANES), per_b3(t_pad, LANES), per_b3(t_pad, LANES)]),
        out_shape=[jax.ShapeDtypeStruct((bd, t_pad, past + LANES), I32),
                   jax.ShapeDtypeStruct((bd, t_pad, LANES), I32),
                   jax.ShapeDtypeStruct((bd, t_pad, LANES), I32)],
        compiler_params=_cparams(("parallel", "arbitrary")),
        name="dsa_sample_select",
    )(page_table, qi_st, wi_st, ki_new, *([cache_ik] * n_pp))

    per_b4 = pl.BlockSpec((None, AT_KV_HEADS, rows_a, HEAD_DIM), lambda bi, si, pt: (bi, 0, 0, 0))
    return pl.pallas_call(
        functools.partial(_dsa_sample_attn_kernel, n_pp=n_pp, n_steps=n_steps, t_pad=t_pad),
        grid_spec=pltpu.PrefetchScalarGridSpec(
            num_scalar_prefetch=1,
            grid=(bd, n_steps),
            in_specs=[per_b4,
                      pl.BlockSpec((None, t_pad, n_pp * PAGE_SIZE), lambda bi, si, pt: (bi, 0, si)),
                      pl.BlockSpec((None, t_pad, LANES), lambda bi, si, pt: (bi, 0, past // LANES)),
                      per_b3(t_pad, LANES), per_b3(t_pad, LANES),
                      per_b3(PAGE_SIZE, KV_W), per_b3(PAGE_SIZE, KV_W), per_b4]
                     + [kv_page_spec(j) for j in range(n_pp)]
                     + [kv_page_spec(j) for j in range(n_pp)],
            out_specs=per_b4,
            scratch_shapes=[pltpu.VMEM((AT_KV_HEADS, rows_a, LANES), F32),
                            pltpu.VMEM((AT_KV_HEADS, rows_a, LANES), F32),
                            pltpu.VMEM((AT_KV_HEADS, rows_a, HEAD_DIM), F32)]),
        out_shape=jax.ShapeDtypeStruct((bd, AT_KV_HEADS, rows_a, HEAD_DIM), F32),
        compiler_params=_cparams(("parallel", "arbitrary")),
        name="dsa_sample_attn",
    )(page_table, q_st, keys, keys, thr, cut, k_new, v_new, gc_st, *([cache_k] * n_pp), *([cache_v] * n_pp))


def _row_tile(m):
    return math.gcd(m, 256)


def _even_layer(x3, mk, mv, c0, n0, m0, prm, t_valid):
    norm_g, w_in_bf, bif, hg, lng, lnb, ws, bs, w_out_bf, mqn = prm
    b, t, d = x3.shape
    L = math.gcd(t, LANES)
    x2 = x3.reshape(b * t, d)
    tm = _row_tile(b * t)
    h3 = _in_proj(x2, norm_g, w_in_bf, tm).reshape(b, t, E_COLS)
    m0b = jnp.broadcast_to(m0[:, :, None], (b, ML_HEADS, LANES))
    bs_t = jnp.pad(bs[:, :L].T, ((0, 0), (0, LANES - GM_GROUPS)))
    y, vbn, c, n, mb = _even_mix(h3, c0, n0, m0b, bif, hg.reshape(1, ML_W), lng.reshape(1, GM_W),
                                 lnb.reshape(1, GM_W), ws[:, :L, :L], bs_t, mk, mv, mqn.reshape(1, HEAD_DIM),
                                 L, t_valid)
    x_new = _out_proj(x2, [y.reshape(b * t, -1)], [w_out_bf], tm).reshape(b, t, d)
    return x_new, (c, n, mb[:, :, 0]), vbn


def _odd_common(x3, mk, mv, prm, tabs, v_transposed):
    norm_g, w_in_bf, qn, kn, w_out_c, w_out_x, mqn = prm
    b, t, d = x3.shape
    x2 = x3.reshape(b * t, d)
    tm = _row_tile(b * t)
    h3 = _in_proj(x2, norm_g, w_in_bf, tm).reshape(b, t, O_COLS)
    tq = math.gcd(t, 256)
    pre = _odd_pre(h3, tabs, qn.reshape(1, HEAD_DIM), kn.reshape(1, HEAD_DIM), mk, mv, mqn.reshape(1, HEAD_DIM), tq,
                   v_transposed)
    return x2, tm, h3, pre


def _finish_odd(x2, tm, yc2, yx, prm, shape):
    w_out_c, w_out_x = prm[4], prm[5]
    return _out_proj(x2, [yc2, yx.reshape(x2.shape[0], XA_W)], [w_out_c, w_out_x], tm).reshape(shape)


def kernel(x_prompt, x_sample, state_mlstm_C, state_mlstm_n, state_mlstm_m, cache_attn_k, cache_attn_v,
           cache_idx_k, cache_mem_k, cache_mem_v, page_table, mem_prompt, norm_even, w_in_even, b_if_even,
           mlstm_hnorm, gmlp_ln_g, gmlp_ln_b, gmlp_w, gmlp_b, w_out_even, norm_odd, w_in_odd, attn_qn, attn_kn,
           w_out_odd, w_mem_kv, mem_qn, mem_kn):
    B, S, D = x_prompt.shape
    Bd, T, _ = x_sample.shape
    depth = w_mem_kv.shape[0]
    n_mem = mem_prompt.shape[1]
    past = page_table.shape[1] * PAGE_SIZE
    assert S % LANES == 0 and T <= SAMPLE_PAD

    we = w_in_even
    we = jnp.concatenate([we[..., :2048], we[..., 2056:], we[..., 2048:2056],
                          jnp.zeros(we.shape[:2] + (LANES - 2 * ML_HEADS,), we.dtype)], axis=-1).astype(BF16)
    wo = w_in_odd
    wo = jnp.concatenate([wo[..., 0:1024], wo[..., 1536:2560], wo[..., 1024:1536], wo[..., 2560:3072],
                          wo[..., 3144:4168], wo[..., 3080:3144], wo[..., 3072:3080],
                          jnp.zeros(wo.shape[:2] + (LANES - IDX_DIM - IDX_HEADS,), wo.dtype)], axis=-1).astype(BF16)
    w_out_even_bf = w_out_even.astype(BF16)
    w_out_odd_bf = w_out_odd.astype(BF16)
    bif_pad = jnp.pad(b_if_even, ((0, 0), (0, LANES - 2 * ML_HEADS)))[:, None, :]

    mem_k_p, mem_v_p = _mem_kv(mem_prompt.reshape(B * n_mem, D), w_mem_kv.astype(BF16), mem_kn,
                               _row_tile(B * n_mem))
    mem_k_p = mem_k_p.reshape(depth, B, n_mem, XA_W)
    mem_v_p = mem_v_p.reshape(depth, B, n_mem, XA_W)
    cmk = cache_mem_k.reshape(depth, Bd, n_mem, XA_W)
    cmv = cache_mem_v.reshape(depth, Bd, n_mem, XA_W)

    tp = SAMPLE_PAD
    xp = x_prompt
    xs = jnp.pad(x_sample, ((0, 0), (0, tp - T), (0, 0)))
    tabs_p = _rope_tables(jnp.arange(S))
    tabs_s = _rope_tables(past + jnp.arange(tp))
    zero_c = jnp.zeros((B, ML_HEADS, HEAD_DIM, HEAD_DIM), F32)
    zero_n = jnp.zeros((B, ML_HEADS, HEAD_DIM), F32)
    zero_m = jnp.zeros((B, ML_HEADS), F32)
    topk_p = min(TOPK_MAX, S // 4)
    topk_s = min(TOPK_MAX, (past + T) // 4)

    mlC_p, mln_p, mlm_p, mlC_s, mln_s, mlm_s, gv_s = [], [], [], [], [], [], []
    ak_p, av_p, ik_p, ak_s, av_s, ik_s = [], [], [], [], [], []
    for l in range(depth):
        if l % 2 == 0:
            e = l // 2
            prm = (norm_even[e], we[e], bif_pad[e], mlstm_hnorm[e], gmlp_ln_g[e], gmlp_ln_b[e], gmlp_w[e], gmlp_b[e],
                   w_out_even_bf[e], mem_qn[l])
            xp, (c, n, m), _ = _even_layer(xp, mem_k_p[l], mem_v_p[l], zero_c, zero_n, zero_m, prm, LANES)
            mlC_p.append(c); mln_p.append(n); mlm_p.append(m)
            xs, (c, n, m), vrows = _even_layer(xs, cmk[l], cmv[l], state_mlstm_C[e], state_mlstm_n[e],
                                               state_mlstm_m[e], prm, T)
            mlC_s.append(c); mln_s.append(n); mlm_s.append(m)
            gv_s.append(vrows[:, :T])
        else:
            o = l // 2
            prm = (norm_odd[o], wo[o], attn_qn[o], attn_kn[o], w_out_odd_bf[o, :AT_W], w_out_odd_bf[o, AT_W:],
                   mem_qn[l])
            x2, tm, h3, pre = _odd_common(xp, mem_k_p[l], mem_v_p[l], prm, tabs_p, True)
            q_bf, k, v, k_bf, vt_bf, qi_bf, ki, ki_bf, wi, yx = pre
            yc = _dsa_prompt(q_bf, qi_bf, wi, h3, k_bf, vt_bf, ki_bf, topk_p)
            xp = _finish_odd(x2, tm, yc.reshape(B * S, AT_W), yx, prm, xp.shape)
            ak_p.append(k.reshape(B, S, AT_KV_HEADS, HEAD_DIM))
            av_p.append(v.reshape(B, S, AT_KV_HEADS, HEAD_DIM))
            ik_p.append(ki)
            x2, tm, h3, pre = _odd_common(xs, cmk[l], cmv[l], prm, tabs_s, False)
            q_bf, k, v, k_bf, v_bf, qi_bf, ki, ki_bf, wi, yx = pre
            q_st = q_bf.reshape(Bd, tp, AT_KV_HEADS, AT_GROUP, HEAD_DIM).transpose(0, 2, 3, 1, 4).reshape(
                Bd, AT_KV_HEADS, AT_GROUP * tp, HEAD_DIM)
            gc_st = h3[:, :, O_GC:O_GC + AT_W].reshape(Bd, tp, AT_KV_HEADS, AT_GROUP, HEAD_DIM).transpose(
                0, 2, 3, 1, 4).reshape(Bd, AT_KV_HEADS, AT_GROUP * tp, HEAD_DIM)
            qi_st = qi_bf.reshape(Bd, tp, IDX_HEADS, IDX_DIM).transpose(0, 2, 1, 3).reshape(
                Bd, IDX_HEADS * tp, IDX_DIM)
            wi_h = wi[:, :, IDX_DIM:IDX_DIM + IDX_HEADS] * IDX_DIM ** -0.5
            wi_st = jnp.broadcast_to(wi_h.transpose(0, 2, 1).reshape(Bd, IDX_HEADS * tp, 1),
                                     (Bd, IDX_HEADS * tp, LANES))
            pad_rows = lambda a: jnp.pad(a, ((0, 0), (0, PAGE_SIZE - tp), (0, 0)))
            y_st = _dsa_sample(q_st, qi_st, wi_st, gc_st, pad_rows(k), pad_rows(v), pad_rows(ki_bf),
                               cache_attn_k, cache_attn_v, cache_idx_k, o, page_table, topk_s, tp)
            yc = y_st.reshape(Bd, AT_KV_HEADS, AT_GROUP, tp, HEAD_DIM).transpose(0, 3, 1, 2, 4).reshape(
                Bd * tp, AT_W)
            xs = _finish_odd(x2, tm, yc, yx, prm, xs.shape)
            ak_s.append(k[:, :T].reshape(Bd, T, AT_KV_HEADS, HEAD_DIM))
            av_s.append(v[:, :T].reshape(Bd, T, AT_KV_HEADS, HEAD_DIM))
            ik_s.append(ki[:, :T])
    return (xp, xs[:, :T],
            jnp.stack(mlC_p), jnp.stack(mln_p), jnp.stack(mlm_p),
            jnp.stack(mlC_s), jnp.stack(mln_s), jnp.stack(mlm_s),
            jnp.stack(gv_s),
            jnp.stack(ak_p), jnp.stack(av_p), jnp.stack(ik_p),
            jnp.stack(ak_s), jnp.stack(av_s), jnp.stack(ik_s),
            mem_k_p.reshape(depth, B, n_mem, XA_HEADS, HEAD_DIM),
            mem_v_p.reshape(depth, B, n_mem, XA_HEADS, HEAD_DIM))
```

```python
import functools
import math

import numpy as np
import jax
import jax.numpy as jnp
from jax import lax
from jax.experimental import pallas as pl
from jax.experimental.pallas import tpu as pltpu

F32 = jnp.float32
BF16 = jnp.bfloat16
I32 = jnp.int32

HEAD_DIM = 128
ML_HEADS = 4
GM_GROUPS = 4
AT_HEADS = 8
AT_KV_HEADS = 2
AT_GROUP = AT_HEADS // AT_KV_HEADS
IDX_HEADS = 8
IDX_DIM = 64
XA_HEADS = 4
TOPK_MAX = 256
PAGE_SIZE = 128
ROPE_THETA = 500000.0
ROPE_FRAC = 4
NORM_EPS = 1e-6

ML_W = ML_HEADS * HEAD_DIM
GM_W = GM_GROUPS * HEAD_DIM
XA_W = XA_HEADS * HEAD_DIM
AT_W = AT_HEADS * HEAD_DIM
KV_W = AT_KV_HEADS * HEAD_DIM
IDX_W = IDX_HEADS * IDX_DIM

LANES = 128
SUBLANES = 8
VMEM_LIMIT = 56 * 1024 * 1024
NEG = -1e30
NO_SCORE = -3.0e38
BISECT_STEPS = 16
SAMPLE_PAD = SUBLANES
BF16_SUBLANES = 16
VT_ROWS = HEAD_DIM + BF16_SUBLANES
Q_SCALE = HEAD_DIM ** -0.5 * math.log2(math.e)
PAGES_PER_STEP = 16
COUNT_ACC_TILES = 4
GROUP_SHIFT = AT_GROUP.bit_length() - 1
assert AT_GROUP == 1 << GROUP_SHIFT and AT_KV_HEADS & (AT_KV_HEADS - 1) == 0

E_Q, E_K, E_V, E_O, E_GA, E_U, E_VB, E_GB, E_XQ, E_GX, E_IF = (
    0, 512, 1024, 1536, 2048, 2560, 3072, 3584, 4096, 4608, 5120)
E_MAIN = 5120
E_COLS = E_MAIN + LANES
O_Q, O_GC, O_K, O_V, O_QI, O_XQ, O_GX, O_KI = 0, 1024, 2048, 2304, 2560, 3072, 3584, 4096
O_MAIN = 4096
O_COLS = O_MAIN + LANES


def _cparams(sem):
    return pltpu.CompilerParams(dimension_semantics=sem, vmem_limit_bytes=VMEM_LIMIT)


def _sigmoid(x):
    return 1.0 / (1.0 + jnp.exp(-x))


def _silu(x):
    return x * _sigmoid(x)


def _rms_rows(x, g):
    return x * lax.rsqrt(jnp.mean(x * x, axis=-1, keepdims=True) + NORM_EPS) * g


def _dot_nt(a, b):
    return lax.dot_general(a, b, (((1,), (1,)), ((), ())), preferred_element_type=F32)


def _dot_tn(a, b):
    return lax.dot_general(a, b, (((0,), (0,)), ((), ())), preferred_element_type=F32)


def _dot(a, b):
    return jnp.dot(a, b, preferred_element_type=F32)


def _in_proj_kernel(x_ref, g_ref, w_ref, o_ref, *, col_chunk):
    xn = _rms_rows(x_ref[...], g_ref[...]).astype(BF16)
    n = o_ref.shape[-1]
    for c0 in range(0, n, col_chunk):
        c1 = min(c0 + col_chunk, n)
        o_ref[:, c0:c1] = _dot(xn, w_ref[:, c0:c1])


def _in_proj(x2, g, w_bf, tm):
    m, d = x2.shape
    n = w_bf.shape[1]
    return pl.pallas_call(
        functools.partial(_in_proj_kernel, col_chunk=512),
        grid=(m // tm,),
        in_specs=[pl.BlockSpec((tm, d), lambda i: (i, 0)),
                  pl.BlockSpec((1, d), lambda i: (0, 0)),
                  pl.BlockSpec((d, n), lambda i: (0, 0))],
        out_specs=pl.BlockSpec((tm, n), lambda i: (i, 0)),
        out_shape=jax.ShapeDtypeStruct((m, n), F32),
        compiler_params=_cparams(("parallel",)),
        name="in_proj",
    )(x2, g.reshape(1, d), w_bf)


def _out_proj_kernel(*refs, n_in):
    x_ref = refs[0]
    y_refs = refs[1:1 + n_in]
    w_refs = refs[1 + n_in:1 + 2 * n_in]
    o_ref = refs[1 + 2 * n_in]
    acc = x_ref[...]
    for y_ref, w_ref in zip(y_refs, w_refs):
        acc = acc + _dot(y_ref[...].astype(BF16), w_ref[...])
    o_ref[...] = acc


def _out_proj(x2, ys, ws, tm):
    m, d = x2.shape
    n_in = len(ys)
    in_specs = [pl.BlockSpec((tm, d), lambda i: (i, 0))]
    in_specs += [pl.BlockSpec((tm, y.shape[1]), lambda i: (i, 0)) for y in ys]
    in_specs += [pl.BlockSpec(w.shape, lambda i: (0, 0)) for w in ws]
    return pl.pallas_call(
        functools.partial(_out_proj_kernel, n_in=n_in),
        grid=(m // tm,),
        in_specs=in_specs,
        out_specs=pl.BlockSpec((tm, d), lambda i: (i, 0)),
        out_shape=jax.ShapeDtypeStruct((m, d), F32),
        compiler_params=_cparams(("parallel",)),
        name="out_proj",
    )(x2, *ys, *ws)


def _mem_kv_kernel(mem_ref, w_ref, kn_ref, k_ref, v_ref):
    kv = _dot(mem_ref[...].astype(BF16), w_ref[...])
    kn = kn_ref[...]
    for h in range(XA_HEADS):
        sl = slice(h * HEAD_DIM, (h + 1) * HEAD_DIM)
        k_ref[:, sl] = _rms_rows(kv[:, sl], kn)
    v_ref[...] = kv[:, XA_W:]


def _mem_kv(mem2, w_kv_bf, mem_kn, tm):
    m, d = mem2.shape
    depth = w_kv_bf.shape[0]
    out = jax.ShapeDtypeStruct((depth, m, XA_W), F32)
    return pl.pallas_call(
        _mem_kv_kernel,
        grid=(depth, m // tm),
        in_specs=[pl.BlockSpec((tm, d), lambda l, i: (i, 0)),
                  pl.BlockSpec((None, d, 2 * XA_W), lambda l, i: (l, 0, 0)),
                  pl.BlockSpec((None, 1, HEAD_DIM), lambda l, i: (l, 0, 0))],
        out_specs=[pl.BlockSpec((None, tm, XA_W), lambda l, i: (l, i, 0)),
                   pl.BlockSpec((None, tm, XA_W), lambda l, i: (l, i, 0))],
        out_shape=[out, out],
        compiler_params=_cparams(("parallel", "parallel")),
        name="mem_kv",
    )(mem2, w_kv_bf, mem_kn.reshape(depth, 1, HEAD_DIM))


def _mem_attn_heads(xq, gx, mk_ref, mv_ref, qn):
    outs = []
    for h in range(XA_HEADS):
        sl = slice(h * HEAD_DIM, (h + 1) * HEAD_DIM)
        q = _rms_rows(xq[:, sl], qn).astype(BF16)
        s = _dot_nt(q, mk_ref[:, sl].astype(BF16)) * HEAD_DIM ** -0.5
        s = s - jnp.max(s, axis=-1, keepdims=True)
        p = jnp.exp(s)
        p = p / jnp.sum(p, axis=-1, keepdims=True)
        o = _dot(p.astype(BF16), mv_ref[:, sl].astype(BF16))
        outs.append(o * _silu(gx[:, sl]))
    return outs


def _even_mix_kernel(h_ref, if_ref, c0_ref, n0_ref, m0_ref, bif_ref, hg_ref, lng_ref, lnb_ref,
                     ws_ref, bs_ref, mk_ref, mv_ref, qn_ref,
                     y_ref, vbn_ref, c_ref, n_ref, m_ref, *, t_valid):
    L = h_ref.shape[0]
    ci = pl.program_id(1)

    @pl.when(ci == 0)
    def _():
        c_ref[...] = c0_ref[...]
        n_ref[...] = n0_ref[...]
        m_ref[...] = m0_ref[...]

    row = lax.broadcasted_iota(I32, (L, L), 0)
    col = lax.broadcasted_iota(I32, (L, L), 1)
    eye = row == col
    row_le_col = row <= col
    col_le_row = col <= row
    tcol = lax.broadcasted_iota(I32, (L, 1), 0)

    gates = if_ref[...] + bif_ref[...]
    log_f_all = jnp.minimum(gates, 0.0) - jnp.log(1.0 + jnp.exp(-jnp.abs(gates)))

    for h in range(ML_HEADS):
        sl = slice(h * HEAD_DIM, (h + 1) * HEAD_DIM)
        li_col = gates[:, h:h + 1]
        lf_col = log_f_all[:, ML_HEADS + h:ML_HEADS + h + 1]
        if t_valid < L:
            li_col = jnp.where(tcol < t_valid, li_col, NEG)
            lf_col = jnp.where(tcol < t_valid, lf_col, 0.0)
        li_b = jnp.broadcast_to(li_col, (L, L))
        lf_b = jnp.broadcast_to(lf_col, (L, L))
        li_row = jnp.sum(jnp.where(eye, li_b, 0.0), axis=0, keepdims=True)
        lf_row = jnp.sum(jnp.where(eye, lf_b, 0.0), axis=0, keepdims=True)
        b_row = jnp.sum(jnp.where(row_le_col, lf_b, 0.0), axis=0, keepdims=True)
        b_col = jnp.sum(jnp.where(col_le_row, jnp.broadcast_to(lf_row, (L, L)), 0.0),
                        axis=1, keepdims=True)
        m_prev = m_ref[h:h + 1, 0:1]
        n_prev = n_ref[h:h + 1, :]
        c_prev = c_ref[h]

        dlog = jnp.where(col_le_row, b_col - b_row + li_row, NEG)
        inter = b_col + m_prev
        m_t = jnp.maximum(inter, jnp.max(dlog, axis=1, keepdims=True))
        w_intra = jnp.exp(dlog - m_t)
        w_inter = jnp.exp(inter - m_t)
        q = h_ref[:, E_Q + h * HEAD_DIM:E_Q + (h + 1) * HEAD_DIM]
        k = h_ref[:, E_K + h * HEAD_DIM:E_K + (h + 1) * HEAD_DIM] * HEAD_DIM ** -0.5
        v = h_ref[:, E_V + h * HEAD_DIM:E_V + (h + 1) * HEAD_DIM]
        q_bf = q.astype(BF16)
        k_bf = k.astype(BF16)
        qk = _dot_nt(q_bf, k_bf) * w_intra
        num = _dot(qk.astype(BF16), v.astype(BF16)) + w_inter * _dot(q_bf, c_prev.astype(BF16))
        den = jnp.sum(qk, axis=1, keepdims=True) + w_inter * jnp.sum(q * n_prev, axis=1, keepdims=True)
        hh = num / jnp.maximum(jnp.abs(den), jnp.exp(-m_t))

        b_end = b_col[L - 1:L, :]
        g = b_end - b_col + li_col
        m_new = jnp.maximum(b_end + m_prev, jnp.max(g, axis=0, keepdims=True))
        a_prev = jnp.exp(b_end + m_prev - m_new)
        a_tok = jnp.exp(g - m_new)
        c_ref[h] = a_prev * c_prev + _dot_tn(k_bf, (a_tok * v).astype(BF16))
        n_ref[h:h + 1, :] = a_prev * n_prev + jnp.sum(a_tok * k, axis=0, keepdims=True)
        m_ref[h:h + 1, :] = jnp.broadcast_to(m_new, (1, LANES))

        ha = _rms_rows(hh, hg_ref[:, sl])
        y_ref[:, sl] = ha * _sigmoid(h_ref[:, E_O + h * HEAD_DIM:E_O + (h + 1) * HEAD_DIM]) * _silu(
            h_ref[:, E_GA + h * HEAD_DIM:E_GA + (h + 1) * HEAD_DIM])

    vb = h_ref[:, E_VB:E_VB + GM_W]
    vc = vb - jnp.mean(vb, axis=-1, keepdims=True)
    var = jnp.mean(vc * vc, axis=-1, keepdims=True)
    vbn = vc * lax.rsqrt(var + NORM_EPS) * lng_ref[...] + lnb_ref[...]
    vbn_ref[...] = vbn
    for g_i in range(GM_GROUPS):
        sl = slice(g_i * HEAD_DIM, (g_i + 1) * HEAD_DIM)
        w = jnp.where(col_le_row, ws_ref[g_i], 0.0).astype(BF16)
        mixed = _dot(w, vbn[:, sl].astype(BF16)) + bs_ref[:, g_i:g_i + 1]
        y_ref[:, ML_W + g_i * HEAD_DIM:ML_W + (g_i + 1) * HEAD_DIM] = (
            h_ref[:, E_U + g_i * HEAD_DIM:E_U + (g_i + 1) * HEAD_DIM] * mixed
            * _silu(h_ref[:, E_GB + g_i * HEAD_DIM:E_GB + (g_i + 1) * HEAD_DIM]))

    outs = _mem_attn_heads(h_ref[:, E_XQ:E_XQ + XA_W], h_ref[:, E_GX:E_GX + XA_W], mk_ref, mv_ref, qn_ref[...])
    for h, o in enumerate(outs):
        y_ref[:, ML_W + GM_W + h * HEAD_DIM:ML_W + GM_W + (h + 1) * HEAD_DIM] = o


def _even_mix(h3, c0, n0, m0b, bif, hg, lng, lnb, ws, bs_t, mk, mv, qn, L, t_valid):
    b, t, _ = h3.shape
    nc = t // L
    n_mem = mk.shape[1]
    full2 = lambda bi, ci: (0, 0)
    return pl.pallas_call(
        functools.partial(_even_mix_kernel, t_valid=t_valid),
        grid=(b, nc),
        in_specs=[pl.BlockSpec((None, L, E_MAIN), lambda bi, ci: (bi, ci, 0)),
                  pl.BlockSpec((None, L, LANES), lambda bi, ci: (bi, ci, E_MAIN // LANES)),
                  pl.BlockSpec((None, ML_HEADS, HEAD_DIM, HEAD_DIM), lambda bi, ci: (bi, 0, 0, 0)),
                  pl.BlockSpec((None, ML_HEADS, HEAD_DIM), lambda bi, ci: (bi, 0, 0)),
                  pl.BlockSpec((None, ML_HEADS, LANES), lambda bi, ci: (bi, 0, 0)),
                  pl.BlockSpec((1, LANES), full2),
                  pl.BlockSpec((1, ML_W), full2),
                  pl.BlockSpec((1, GM_W), full2),
                  pl.BlockSpec((1, GM_W), full2),
                  pl.BlockSpec((GM_GROUPS, L, L), lambda bi, ci: (0, 0, 0)),
                  pl.BlockSpec((L, LANES), full2),
                  pl.BlockSpec((None, n_mem, XA_W), lambda bi, ci: (bi, 0, 0)),
                  pl.BlockSpec((None, n_mem, XA_W), lambda bi, ci: (bi, 0, 0)),
                  pl.BlockSpec((1, HEAD_DIM), full2)],
        out_specs=[pl.BlockSpec((None, L, ML_W + GM_W + XA_W), lambda bi, ci: (bi, ci, 0)),
                   pl.BlockSpec((None, L, GM_W), lambda bi, ci: (bi, ci, 0)),
                   pl.BlockSpec((None, ML_HEADS, HEAD_DIM, HEAD_DIM), lambda bi, ci: (bi, 0, 0, 0)),
                   pl.BlockSpec((None, ML_HEADS, HEAD_DIM), lambda bi, ci: (bi, 0, 0)),
                   pl.BlockSpec((None, ML_HEADS, LANES), lambda bi, ci: (bi, 0, 0))],
        out_shape=[jax.ShapeDtypeStruct((b, t, ML_W + GM_W + XA_W), F32),
                   jax.ShapeDtypeStruct((b, t, GM_W), F32),
                   jax.ShapeDtypeStruct((b, ML_HEADS, HEAD_DIM, HEAD_DIM), F32),
                   jax.ShapeDtypeStruct((b, ML_HEADS, HEAD_DIM), F32),
                   jax.ShapeDtypeStruct((b, ML_HEADS, LANES), F32)],
        compiler_params=_cparams(("parallel", "arbitrary")),
        name="even_mix",
    )(h3, h3, c0, n0, m0b, bif, hg, lng, lnb, ws, bs_t, mk, mv, qn)


def _rope_lanes(x, cos, sin_lo, sin_hi, half):
    n = x.shape[-1]
    return x * cos + pltpu.roll(x, n - half, 1) * sin_lo + pltpu.roll(x, half, 1) * sin_hi


def _odd_pre_kernel(h_ref, kiwi_ref, ca_ref, sa1_ref, sa2_ref, ci_ref, si1_ref, si2_ref,
                    qn_ref, kn_ref, mk_ref, mv_ref, mqn_ref,
                    q_ref, k_ref, v_ref, kbf_ref, vbf_ref, qi_ref, ki_ref, kibf_ref, wi_ref, yx_ref, *, v_transposed):
    ca, sa1, sa2 = ca_ref[...], sa1_ref[...], sa2_ref[...]
    ci, si1, si2 = ci_ref[...], si1_ref[...], si2_ref[...]
    a_half = HEAD_DIM // ROPE_FRAC // 2
    i_half = IDX_DIM // ROPE_FRAC // 2
    for h in range(AT_HEADS):
        sl = slice(O_Q + h * HEAD_DIM, O_Q + (h + 1) * HEAD_DIM)
        qh = _rope_lanes(_rms_rows(h_ref[:, sl], qn_ref[...]), ca, sa1, sa2, a_half)
        q_ref[:, h * HEAD_DIM:(h + 1) * HEAD_DIM] = (qh * Q_SCALE).astype(BF16)
    for h in range(AT_KV_HEADS):
        sl = slice(O_K + h * HEAD_DIM, O_K + (h + 1) * HEAD_DIM)
        kh = _rope_lanes(_rms_rows(h_ref[:, sl], kn_ref[...]), ca, sa1, sa2, a_half)
        k_ref[:, h * HEAD_DIM:(h + 1) * HEAD_DIM] = kh
        kbf_ref[:, h * HEAD_DIM:(h + 1) * HEAD_DIM] = kh.astype(BF16)
    v = h_ref[:, O_V:O_V + KV_W]
    v_ref[...] = v
    if v_transposed:
        tail_row = lax.broadcasted_iota(I32, (VT_ROWS - HEAD_DIM, v.shape[0]), 0)
        tail = jnp.where(tail_row == 0, 1.0, 0.0).astype(BF16)
        for n in range(AT_KV_HEADS):
            vbf_ref[n, :HEAD_DIM, :] = v[:, n * HEAD_DIM:(n + 1) * HEAD_DIM].T.astype(BF16)
            vbf_ref[n, HEAD_DIM:, :] = tail
    else:
        vbf_ref[...] = v.astype(BF16)
    for j in range(IDX_W // LANES):
        sl = slice(O_QI + j * LANES, O_QI + (j + 1) * LANES)
        qi_ref[:, j * LANES:(j + 1) * LANES] = _rope_lanes(h_ref[:, sl], ci, si1, si2, i_half).astype(BF16)
    kiwi = kiwi_ref[...]
    ki = _rope_lanes(kiwi, ci, si1, si2, i_half)[:, :IDX_DIM]
    ki_ref[...] = ki
    kibf_ref[...] = ki.astype(BF16)
    lane = lax.broadcasted_iota(I32, kiwi.shape, 1)
    wsel = (lane >= IDX_DIM) & (lane < IDX_DIM + IDX_HEADS)
    wi_ref[...] = jnp.where(wsel, kiwi * IDX_HEADS ** -0.5, 0.0)
    outs = _mem_attn_heads(h_ref[:, O_XQ:O_XQ + XA_W], h_ref[:, O_GX:O_GX + XA_W], mk_ref, mv_ref, mqn_ref[...])
    for h, o in enumerate(outs):
        yx_ref[:, h * HEAD_DIM:(h + 1) * HEAD_DIM] = o


def _odd_pre(h3, tabs, qn, kn, mk, mv, mqn, tq, v_transposed):
    b, t, _ = h3.shape
    n_mem = mk.shape[1]
    row_blk = lambda w: pl.BlockSpec((None, tq, w), lambda bi, ti: (bi, ti, 0))
    tab = pl.BlockSpec((tq, LANES), lambda bi, ti: (ti, 0))
    vec = pl.BlockSpec((1, HEAD_DIM), lambda bi, ti: (0, 0))
    mem = pl.BlockSpec((None, n_mem, XA_W), lambda bi, ti: (bi, 0, 0))
    sds = lambda w, dt: jax.ShapeDtypeStruct((b, t, w), dt)
    if v_transposed:
        vbf_spec = pl.BlockSpec((None, AT_KV_HEADS, VT_ROWS, tq), lambda bi, ti: (bi, 0, 0, ti))
        vbf_shape = jax.ShapeDtypeStruct((b, AT_KV_HEADS, VT_ROWS, t), BF16)
    else:
        vbf_spec, vbf_shape = row_blk(KV_W), sds(KV_W, BF16)
    return pl.pallas_call(
        functools.partial(_odd_pre_kernel, v_transposed=v_transposed),
        grid=(b, t // tq),
        in_specs=[row_blk(O_MAIN),
                  pl.BlockSpec((None, tq, LANES), lambda bi, ti: (bi, ti, O_MAIN // LANES)),
                  tab, tab, tab, tab, tab, tab, vec, vec, mem, mem, vec],
        out_specs=[row_blk(AT_W), row_blk(KV_W), row_blk(KV_W), row_blk(KV_W), vbf_spec,
                   row_blk(IDX_W), row_blk(IDX_DIM), row_blk(IDX_DIM), row_blk(LANES), row_blk(XA_W)],
        out_shape=[sds(AT_W, BF16), sds(KV_W, F32), sds(KV_W, F32), sds(KV_W, BF16), vbf_shape,
                   sds(IDX_W, BF16), sds(IDX_DIM, F32), sds(IDX_DIM, BF16), sds(LANES, F32), sds(XA_W, F32)],
        compiler_params=_cparams(("parallel", "parallel")),
        name="odd_pre",
    )(h3, h3, *tabs, qn, kn, mk, mv, mqn)


def _rope_tables(pos):
    pos = pos.astype(F32)[:, None]

    def build(width, reps):
        rd = width // ROPE_FRAC
        half = rd // 2
        inv = ROPE_THETA ** (-jnp.arange(half, dtype=F32) / half)
        ang = pos * inv[None, :]
        cos, sin = jnp.cos(ang), jnp.sin(ang)
        ones = jnp.ones((pos.shape[0], width - rd), F32)
        zeros_h = jnp.zeros((pos.shape[0], half), F32)
        zeros_r = jnp.zeros((pos.shape[0], width - rd), F32)
        c = jnp.concatenate([cos, cos, ones], axis=1)
        s_lo = jnp.concatenate([-sin, zeros_h, zeros_r], axis=1)
        s_hi = jnp.concatenate([zeros_h, sin, zeros_r], axis=1)
        return [jnp.tile(a, (1, reps)) for a in (c, s_lo, s_hi)]

    return build(HEAD_DIM, 1) + build(IDX_DIM, LANES // IDX_DIM)


def _topk_threshold(score_ref, idx, n_chunks, chunk_shape, key_axis, k, pos_bits):
    red_shape = tuple(1 if a == key_axis else d for a, d in enumerate(chunk_shape))
    pos_in_chunk = lax.broadcasted_iota(I32, chunk_shape, key_axis)
    acc_rows = min(chunk_shape[0], COUNT_ACC_TILES * SUBLANES) if key_axis == 0 else chunk_shape[0]
    acc_shape = (acc_rows, chunk_shape[1])
    zeros = jnp.zeros(acc_shape, F32)

    def folded(x, op=jnp.add):
        parts = [x[r:r + acc_rows] for r in range(0, chunk_shape[0], acc_rows)]
        while len(parts) > 1:
            parts = [op(a, b) for a, b in zip(parts[::2], parts[1::2])] + ([parts[-1]] if len(parts) % 2 else [])
        return parts[0]

    def count(pred):
        def body(c, acc):
            return acc + folded(jnp.where(pred(score_ref[idx(c)], c), 1.0, 0.0))

        return jnp.sum(lax.fori_loop(0, n_chunks, body, zeros), axis=key_axis, keepdims=True)

    def count_ge(t):
        t_b = jnp.broadcast_to(t, chunk_shape)
        return count(lambda s, c: s >= t_b)

    def range_body(c, carry):
        top, bottom, valid = carry
        s = score_ref[idx(c)]
        ok = s > NO_SCORE
        return (jnp.maximum(top, folded(s, jnp.maximum)),
                jnp.minimum(bottom, folded(jnp.where(ok, s, -NO_SCORE), jnp.minimum)),
                valid + folded(jnp.where(ok, 1.0, 0.0)))

    top, bottom, valid = lax.fori_loop(0, n_chunks, range_body,
                                       (jnp.full(acc_shape, NO_SCORE, F32), jnp.full(acc_shape, -NO_SCORE, F32), zeros))
    top = jnp.max(top, axis=key_axis, keepdims=True)
    bottom = jnp.min(bottom, axis=key_axis, keepdims=True)
    rank = jnp.minimum(jnp.sum(valid, axis=key_axis, keepdims=True), k)

    def bisect_step(i, carry):
        lo, mid_hi, hi = carry
        mid = 0.5 * (lo + mid_hi)
        enough = count_ge(mid) >= rank
        return jnp.where(enough, mid, lo), jnp.where(enough, mid_hi, mid), jnp.where(enough, hi, mid)

    _, _, hi = lax.fori_loop(0, BISECT_STEPS, bisect_step, (bottom, top, jnp.full(red_shape, -NO_SCORE, F32)))

    def walk_cond(carry):
        return jnp.max(carry[2]) > 0.0

    def walk_body(carry):
        hi, thr, todo = carry
        hi_b = jnp.broadcast_to(hi, chunk_shape)

        def body(c, acc):
            s = score_ref[idx(c)]
            return jnp.maximum(acc, folded(jnp.where(s < hi_b, s, NO_SCORE), jnp.maximum))

        cand = jnp.max(lax.fori_loop(0, n_chunks, body, jnp.full(acc_shape, NO_SCORE, F32)),
                       axis=key_axis, keepdims=True)
        found = count_ge(cand) >= rank
        open_ = todo > 0.0
        return (jnp.where(open_, cand, hi), jnp.where(open_, cand, thr),
                jnp.where(open_ & jnp.logical_not(found), 1.0, 0.0))

    _, thr, _ = lax.while_loop(walk_cond, walk_body, (hi, bottom, jnp.ones(red_shape, F32)))
    thr_b = jnp.broadcast_to(thr, chunk_shape)

    def count_body(c, carry):
        gt, eq = carry
        s = score_ref[idx(c)]
        return (gt + folded(jnp.where(s > thr_b, 1.0, 0.0)), eq + folded(jnp.where(s == thr_b, 1.0, 0.0)))

    gt, eq = lax.fori_loop(0, n_chunks, count_body, (zeros, zeros))
    need = rank - jnp.sum(gt, axis=key_axis, keepdims=True)
    excess = jnp.sum(eq, axis=key_axis, keepdims=True) - need

    def pos_step(i, cut):
        cand = cut + lax.shift_left(jnp.int32(1), pos_bits - 1 - i)
        cand_b = jnp.broadcast_to(cand, chunk_shape)
        cnt = count(lambda s, c: (s == thr_b) & (c * chunk_shape[key_axis] + pos_in_chunk < cand_b))
        return jnp.where(cnt < need, cand, cut)

    any_excess = jnp.max(excess) > 0.0
    cut = lax.cond(any_excess,
                   lambda: lax.fori_loop(0, pos_bits, pos_step, jnp.zeros(red_shape, I32)),
                   lambda: jnp.full(red_shape, 2 ** pos_bits, I32))
    return thr, cut, any_excess


def _selected(scores, thr, cut, pos):
    return (scores > thr) | ((scores == thr) & (pos <= cut))


def _dsa_prompt_kernel(q_ref, qi_ref, wi_ref, gc_ref, k_ref, vt_ref, ki_ref, o_ref,
                       sc_ref, qt_ref, qit_ref, wit_ref, m_ref, l_ref, acc_ref, *, topk, pos_bits, ck, ca):
    qb = pl.program_id(1)
    nq = q_ref.shape[0]
    n_chunks = (qb * nq) // ck + 1
    chunk = (ck, LANES)
    kpos_in = lax.broadcasted_iota(I32, chunk, 0)
    qpos = qb * nq + lax.broadcasted_iota(I32, chunk, 1)

    for h in range(AT_HEADS):
        sl = slice(h * HEAD_DIM, (h + 1) * HEAD_DIM)
        qt_ref[:, sl] = q_ref[:, sl].astype(F32).T.astype(BF16)
    for j in range(IDX_W // LANES):
        t = qi_ref[:, j * LANES:(j + 1) * LANES].astype(F32).T.astype(BF16)
        for i in range(LANES // IDX_DIM):
            h = j * (LANES // IDX_DIM) + i
            qit_ref[:, h * LANES:(h + 1) * LANES] = t[i * IDX_DIM:(i + 1) * IDX_DIM, :]
    wit_ref[...] = wi_ref[...].T[IDX_DIM:IDX_DIM + IDX_HEADS, :] * IDX_DIM ** -0.5

    def score_body(c, carry):
        off = pl.multiple_of(c * ck, ck)
        sc = _dot(ki_ref[pl.ds(off, ck), :], qit_ref[...])
        acc = jnp.zeros(chunk, F32)
        for h in range(IDX_HEADS):
            acc = acc + jnp.maximum(sc[:, h * LANES:(h + 1) * LANES], 0.0) * wit_ref[h:h + 1, :]
        sc_ref[pl.ds(off, ck), :] = jnp.where(off + kpos_in <= qpos, acc, NO_SCORE)
        return carry

    lax.fori_loop(0, n_chunks, score_body, 0)

    n_att = (qb * nq) // ca + 1

    @pl.when(n_att * ca > n_chunks * ck)
    def _():
        sc_ref[pl.ds(pl.multiple_of(n_chunks * ck, ck), ck), :] = jnp.full(chunk, NO_SCORE, F32)

    def idx(c):
        return (pl.ds(pl.multiple_of(c * ca, ca), ca), slice(None))

    thr, cut, any_excess = _topk_threshold(sc_ref, idx, n_att, (ca, LANES), 0, float(topk), pos_bits)

    @pl.when(any_excess)
    def _():
        def demote_body(c, carry):
            off = pl.multiple_of(c * ck, ck)
            sc = sc_ref[pl.ds(off, ck), :]
            sc_ref[pl.ds(off, ck), :] = jnp.where((sc == thr) & (off + kpos_in > cut), NO_SCORE, sc)
            return carry

        lax.fori_loop(0, n_chunks, demote_body, 0)

    thr_eff = jnp.broadcast_to(thr, (ca, LANES))

    m_ref[...] = jnp.full(m_ref.shape, NEG, F32)
    l_ref[...] = jnp.zeros(l_ref.shape, F32)
    acc_ref[...] = jnp.zeros(acc_ref.shape, F32)
    pair = 2 * HEAD_DIM
    n_pairs = AT_HEADS // 2

    def attn_body(c, carry):
        off = pl.multiple_of(c * ca, ca)
        bias = jnp.where(sc_ref[pl.ds(off, ca), :] >= thr_eff, 0.0, NEG)
        s = jnp.concatenate(
            [_dot(k_ref[pl.ds(off, ca), ((2 * hp) // AT_GROUP) * HEAD_DIM:((2 * hp) // AT_GROUP + 1) * HEAD_DIM],
                  qt_ref[:, hp * pair:(hp + 1) * pair]) for hp in range(n_pairs)], axis=1)
        s = s + jnp.concatenate([bias] * AT_HEADS, axis=1)
        m_old = m_ref[...]
        m_new = jnp.maximum(m_old, jnp.max(s, axis=0, keepdims=True))
        p = jnp.exp2(s - m_new).astype(BF16)
        alpha = jnp.exp2(m_old - m_new)
        m_ref[...] = m_new
        for hp in range(n_pairs):
            n = (2 * hp) // AT_GROUP
            sl = slice(hp * pair, (hp + 1) * pair)
            pv = _dot(vt_ref[n, :, pl.ds(off, ca)], p[:, sl])
            acc_ref[:, sl] = alpha[:, sl] * acc_ref[:, sl] + pv[:HEAD_DIM]
            l_ref[:, sl] = alpha[:, sl] * l_ref[:, sl] + pv[HEAD_DIM:HEAD_DIM + 1]
        return carry

    lax.fori_loop(0, n_att, attn_body, 0)
    for h in range(AT_HEADS):
        sl = slice(h * HEAD_DIM, (h + 1) * HEAD_DIM)
        o_ref[:, sl] = (acc_ref[:, sl] / l_ref[:, sl]).T * _silu(gc_ref[:, sl])


def _dsa_prompt(q_bf, qi_bf, wi, h3, k_bf, vt_bf, ki_bf, topk):
    b, s, _ = q_bf.shape
    nq = LANES
    ca = math.gcd(s, 4 * LANES)
    ck = max(ca // 2, LANES)
    assert s % ca == 0 and ca % ck == 0 and ca // ck <= 2
    pos_bits = max(1, int(math.ceil(math.log2(s))))
    qblk = lambda w: pl.BlockSpec((None, nq, w), lambda bi, qi: (bi, qi, 0))
    full = lambda w: pl.BlockSpec((None, s, w), lambda bi, qi: (bi, 0, 0))
    return pl.pallas_call(
        functools.partial(_dsa_prompt_kernel, topk=topk, pos_bits=pos_bits, ck=ck, ca=ca),
        grid=(b, s // nq),
        in_specs=[qblk(AT_W), qblk(IDX_W), qblk(LANES),
                  pl.BlockSpec((None, nq, AT_W), lambda bi, qi: (bi, qi, O_GC // AT_W)),
                  full(KV_W),
                  pl.BlockSpec((None, AT_KV_HEADS, VT_ROWS, s), lambda bi, qi: (bi, 0, 0, 0)),
                  full(IDX_DIM)],
        out_specs=qblk(AT_W),
        out_shape=jax.ShapeDtypeStruct((b, s, AT_W), F32),
        scratch_shapes=[pltpu.VMEM((s, LANES), F32),
                        pltpu.VMEM((HEAD_DIM, AT_HEADS * nq), BF16),
                        pltpu.VMEM((IDX_DIM, IDX_HEADS * nq), BF16),
                        pltpu.VMEM((IDX_HEADS, nq), F32),
                        pltpu.VMEM((1, AT_HEADS * nq), F32),
                        pltpu.VMEM((1, AT_HEADS * nq), F32),
                        pltpu.VMEM((HEAD_DIM, AT_HEADS * nq), F32)],
        compiler_params=_cparams(("parallel", "arbitrary")),
        name="dsa_prompt",
    )(q_bf, qi_bf, wi, h3, k_bf, vt_bf, ki_bf)


def _dsa_sample_score_kernel(pt_ref, qi_ref, wi_ref, kin_ref, *rest, n_pp, n_steps, t_pad):
    page_refs = rest[:n_pp]
    sc_ref = rest[n_pp]
    step = pl.program_id(1)
    qi = qi_ref[...]
    wi = wi_ref[...]

    def page_scores(ki_bf):
        sc = jnp.maximum(_dot_nt(qi, ki_bf), 0.0) * wi
        acc = jnp.zeros((t_pad, LANES), F32)
        for h in range(IDX_HEADS):
            acc = acc + sc[h * t_pad:(h + 1) * t_pad, :]
        return acc

    for j in range(n_pp):
        off = pl.multiple_of((step * n_pp + j) * PAGE_SIZE, PAGE_SIZE)
        sc_ref[:, pl.ds(off, PAGE_SIZE)] = page_scores(page_refs[j][...].astype(BF16))

    @pl.when(step == n_steps - 1)
    def _():
        past = n_steps * n_pp * PAGE_SIZE
        shape = (t_pad, LANES)
        tok = lax.broadcasted_iota(I32, shape, 0)
        lane = lax.broadcasted_iota(I32, shape, 1)
        sc_ref[:, past:past + LANES] = jnp.where(lane <= tok, page_scores(kin_ref[...]), NO_SCORE)


def _dsa_sample_select_kernel(sc_ref, selx_ref, *, topk, pos_bits):
    rows, width = sc_ref.shape
    n_chunks = width // LANES
    chunk = (rows, LANES)
    xw = AT_KV_HEADS * LANES

    def idx(c):
        return (slice(None), pl.ds(pl.multiple_of(c * LANES, LANES), LANES))

    thr, cut, _ = _topk_threshold(sc_ref, idx, n_chunks, chunk, 1, float(topk), pos_bits)

    lane = lax.broadcasted_iota(I32, chunk, 1)
    src = lax.broadcasted_iota(I32, (LANES, xw), 0)
    dst = lax.broadcasted_iota(I32, (LANES, xw), 1)
    repeat = jnp.where((dst >= src * AT_KV_HEADS) & (dst < (src + 1) * AT_KV_HEADS), 1.0, 0.0).astype(BF16)

    def sel_body(c, carry):
        sel = _selected(sc_ref[idx(c)], thr, cut, c * LANES + lane)
        sel01 = jnp.where(sel, 1.0, 0.0).astype(BF16)
        selx_ref[:, pl.ds(pl.multiple_of(c * xw, xw), xw)] = _dot(sel01, repeat).astype(BF16)
        return carry

    lax.fori_loop(0, n_chunks, sel_body, 0)


def _dsa_sample_attn_kernel(pt_ref, q_ref, selx_ref, selxn_ref, kn_ref, vn_ref, gc_ref, *rest, n_pp, n_steps, n_tok):
    k_refs = rest[:n_pp]
    v_refs = rest[n_pp:2 * n_pp]
    o_ref, m_ref, l_ref, acc_ref = rest[2 * n_pp:2 * n_pp + 4]
    step = pl.program_id(1)
    rows = n_tok * AT_HEADS
    pw = AT_KV_HEADS * PAGE_SIZE
    shape = (rows, LANES)
    row = lax.broadcasted_iota(I32, (rows, pw), 0)
    col = lax.broadcasted_iota(I32, (rows, pw), 1)
    own_head = (lax.shift_right_logical(row, GROUP_SHIFT) & (AT_KV_HEADS - 1)) == (col & (AT_KV_HEADS - 1))
    q = q_ref[...]

    @pl.when(step == 0)
    def _():
        m_ref[...] = jnp.full(m_ref.shape, NEG, F32)
        l_ref[...] = jnp.zeros(l_ref.shape, F32)
        acc_ref[...] = jnp.zeros(acc_ref.shape, F32)

    def attend(blocks):
        parts = []
        for sx, k_ref, _ in blocks:
            sel = jnp.concatenate([jnp.broadcast_to(sx[t:t + 1], (AT_HEADS, pw)) for t in range(n_tok)], axis=0)
            parts.append(_dot_nt(q, k_ref[...].astype(BF16)) + jnp.where((sel > 0.5) & own_head, 0.0, NEG))
        s = jnp.concatenate(parts, axis=1)
        m_old = m_ref[...]
        m_new = jnp.maximum(m_old, jnp.broadcast_to(jnp.max(s, axis=1, keepdims=True), shape))
        p = jnp.exp2(s - m_new[:, 0:1])
        alpha = jnp.exp2(m_old - m_new)
        l_ref[...] = alpha * l_ref[...] + jnp.broadcast_to(jnp.sum(p, axis=1, keepdims=True), shape)
        acc = alpha * acc_ref[...]
        for j, (_, _, v_ref) in enumerate(blocks):
            acc = acc + _dot(p[:, j * pw:(j + 1) * pw].astype(BF16), v_ref[...].astype(BF16))
        acc_ref[...] = acc
        m_ref[...] = m_new

    selx = selx_ref[...].astype(F32)
    attend([(selx[:, j * pw:(j + 1) * pw], k_refs[j], v_refs[j]) for j in range(n_pp)])

    @pl.when(step == n_steps - 1)
    def _():
        attend([(selxn_ref[...].astype(F32), kn_ref, vn_ref)])
        o_ref[...] = acc_ref[...] / l_ref[...] * _silu(gc_ref[...])


def _dsa_sample(q_rows, qi_st, wi_st, gc_rows, k_new, v_new, ki_new, cache_k, cache_v, cache_ik, layer, page_table,
                topk, t_pad, n_tok):
    bd = q_rows.shape[0]
    n_pages = page_table.shape[1]
    n_pp = math.gcd(n_pages, PAGES_PER_STEP)
    n_steps = n_pages // n_pp
    past = n_pages * PAGE_SIZE
    width = past + LANES
    pos_bits = max(1, int(math.ceil(math.log2(width))))
    rows_i = IDX_HEADS * t_pad
    rows_a = AT_HEADS * n_tok
    pw = AT_KV_HEADS * PAGE_SIZE
    ck = cache_k.reshape(cache_k.shape[0], cache_k.shape[1], pw, HEAD_DIM)
    cv = cache_v.reshape(cache_v.shape[0], cache_v.shape[1], pw, HEAD_DIM)

    def page_spec(rows, cols, j):
        return pl.BlockSpec((None, None, rows, cols), lambda bi, si, pt: (layer, pt[bi, si * n_pp + j], 0, 0))

    per_b3 = lambda r, w: pl.BlockSpec((None, r, w), lambda bi, si, pt: (bi, 0, 0))
    scores = pl.pallas_call(
        functools.partial(_dsa_sample_score_kernel, n_pp=n_pp, n_steps=n_steps, t_pad=t_pad),
        grid_spec=pltpu.PrefetchScalarGridSpec(
            num_scalar_prefetch=1,
            grid=(bd, n_steps),
            in_specs=[per_b3(rows_i, IDX_DIM), per_b3(rows_i, LANES), per_b3(PAGE_SIZE, IDX_DIM)]
                     + [page_spec(PAGE_SIZE, IDX_DIM, j) for j in range(n_pp)],
            out_specs=per_b3(t_pad, width)),
        out_shape=jax.ShapeDtypeStruct((bd, t_pad, width), F32),
        compiler_params=_cparams(("parallel", "arbitrary")),
        name="dsa_sample_score",
    )(page_table, qi_st, wi_st, ki_new, *([cache_ik] * n_pp))

    n_q = bd * n_tok
    rb = min(LANES, -(-n_q // SUBLANES) * SUBLANES)
    n_q_pad = -(-n_q // rb) * rb
    scores_q = jnp.pad(scores[:, :n_tok].reshape(n_q, width), ((0, n_q_pad - n_q), (0, 0)),
                       constant_values=NO_SCORE)
    selx = pl.pallas_call(
        functools.partial(_dsa_sample_select_kernel, topk=topk, pos_bits=pos_bits),
        grid=(n_q_pad // rb,),
        in_specs=[pl.BlockSpec((rb, width), lambda i: (i, 0))],
        out_specs=pl.BlockSpec((rb, AT_KV_HEADS * width), lambda i: (i, 0)),
        out_shape=jax.ShapeDtypeStruct((n_q_pad, AT_KV_HEADS * width), BF16),
        compiler_params=_cparams(("parallel",)),
        name="dsa_sample_select",
    )(scores_q)
    selx = selx[:n_q].reshape(bd, n_tok, AT_KV_HEADS * width)

    return pl.pallas_call(
        functools.partial(_dsa_sample_attn_kernel, n_pp=n_pp, n_steps=n_steps, n_tok=n_tok),
        grid_spec=pltpu.PrefetchScalarGridSpec(
            num_scalar_prefetch=1,
            grid=(bd, n_steps),
            in_specs=[per_b3(rows_a, HEAD_DIM),
                      pl.BlockSpec((None, n_tok, n_pp * pw), lambda bi, si, pt: (bi, 0, si)),
                      pl.BlockSpec((None, n_tok, pw), lambda bi, si, pt: (bi, 0, n_pages)),
                      per_b3(pw, HEAD_DIM), per_b3(pw, HEAD_DIM), per_b3(rows_a, HEAD_DIM)]
                     + [page_spec(pw, HEAD_DIM, j) for j in range(n_pp)]
                     + [page_spec(pw, HEAD_DIM, j) for j in range(n_pp)],
            out_specs=per_b3(rows_a, HEAD_DIM),
            scratch_shapes=[pltpu.VMEM((rows_a, LANES), F32),
                            pltpu.VMEM((rows_a, LANES), F32),
                            pltpu.VMEM((rows_a, HEAD_DIM), F32)]),
        out_shape=jax.ShapeDtypeStruct((bd, rows_a, HEAD_DIM), F32),
        compiler_params=_cparams(("parallel", "arbitrary")),
        name="dsa_sample_attn",
    )(page_table, q_rows, selx, selx, k_new, v_new, gc_rows, *([ck] * n_pp), *([cv] * n_pp))


def _row_tile(m):
    return math.gcd(m, 256)


def _even_layer(x3, mk, mv, c0, n0, m0, prm, t_valid):
    norm_g, w_in_bf, bif, hg, lng, lnb, ws, bs, w_out_bf, mqn = prm
    b, t, d = x3.shape
    L = math.gcd(t, LANES)
    x2 = x3.reshape(b * t, d)
    tm = _row_tile(b * t)
    h3 = _in_proj(x2, norm_g, w_in_bf, tm).reshape(b, t, E_COLS)
    m0b = jnp.broadcast_to(m0[:, :, None], (b, ML_HEADS, LANES))
    bs_t = jnp.pad(bs[:, :L].T, ((0, 0), (0, LANES - GM_GROUPS)))
    y, vbn, c, n, mb = _even_mix(h3, c0, n0, m0b, bif, hg.reshape(1, ML_W), lng.reshape(1, GM_W),
                                 lnb.reshape(1, GM_W), ws[:, :L, :L], bs_t, mk, mv, mqn.reshape(1, HEAD_DIM),
                                 L, t_valid)
    x_new = _out_proj(x2, [y.reshape(b * t, -1)], [w_out_bf], tm).reshape(b, t, d)
    return x_new, (c, n, mb[:, :, 0]), vbn


def _odd_common(x3, mk, mv, prm, tabs, v_transposed):
    norm_g, w_in_bf, qn, kn, w_out_c, w_out_x, mqn = prm
    b, t, d = x3.shape
    x2 = x3.reshape(b * t, d)
    tm = _row_tile(b * t)
    h3 = _in_proj(x2, norm_g, w_in_bf, tm).reshape(b, t, O_COLS)
    tq = math.gcd(t, 256)
    pre = _odd_pre(h3, tabs, qn.reshape(1, HEAD_DIM), kn.reshape(1, HEAD_DIM), mk, mv, mqn.reshape(1, HEAD_DIM), tq,
                   v_transposed)
    return x2, tm, h3, pre


def _finish_odd(x2, tm, yc2, yx, prm, shape):
    w_out_c, w_out_x = prm[4], prm[5]
    return _out_proj(x2, [yc2, yx.reshape(x2.shape[0], XA_W)], [w_out_c, w_out_x], tm).reshape(shape)


def kernel(x_prompt, x_sample, state_mlstm_C, state_mlstm_n, state_mlstm_m, cache_attn_k, cache_attn_v,
           cache_idx_k, cache_mem_k, cache_mem_v, page_table, mem_prompt, norm_even, w_in_even, b_if_even,
           mlstm_hnorm, gmlp_ln_g, gmlp_ln_b, gmlp_w, gmlp_b, w_out_even, norm_odd, w_in_odd, attn_qn, attn_kn,
           w_out_odd, w_mem_kv, mem_qn, mem_kn):
    B, S, D = x_prompt.shape
    Bd, T, _ = x_sample.shape
    depth = w_mem_kv.shape[0]
    n_mem = mem_prompt.shape[1]
    past = page_table.shape[1] * PAGE_SIZE
    assert S % LANES == 0 and T <= SAMPLE_PAD

    we = w_in_even
    we = jnp.concatenate([we[..., :2048], we[..., 2056:], we[..., 2048:2056],
                          jnp.zeros(we.shape[:2] + (LANES - 2 * ML_HEADS,), we.dtype)], axis=-1).astype(BF16)
    wo = w_in_odd
    wo = jnp.concatenate([wo[..., 0:1024], wo[..., 1536:2560], wo[..., 1024:1536], wo[..., 2560:3072],
                          wo[..., 3144:4168], wo[..., 3080:3144], wo[..., 3072:3080],
                          jnp.zeros(wo.shape[:2] + (LANES - IDX_DIM - IDX_HEADS,), wo.dtype)], axis=-1).astype(BF16)
    w_out_even_bf = w_out_even.astype(BF16)
    w_out_odd_bf = w_out_odd.astype(BF16)
    bif_pad = jnp.pad(b_if_even, ((0, 0), (0, LANES - 2 * ML_HEADS)))[:, None, :]

    mem_k_p, mem_v_p = _mem_kv(mem_prompt.reshape(B * n_mem, D), w_mem_kv.astype(BF16), mem_kn,
                               _row_tile(B * n_mem))
    mem_k_p = mem_k_p.reshape(depth, B, n_mem, XA_W)
    mem_v_p = mem_v_p.reshape(depth, B, n_mem, XA_W)
    cmk = cache_mem_k.reshape(depth, Bd, n_mem, XA_W)
    cmv = cache_mem_v.reshape(depth, Bd, n_mem, XA_W)

    tp = SAMPLE_PAD
    xp = x_prompt
    xs = jnp.pad(x_sample, ((0, 0), (0, tp - T), (0, 0)))
    tabs_p = _rope_tables(jnp.arange(S))
    tabs_s = _rope_tables(past + jnp.arange(tp))
    zero_c = jnp.zeros((B, ML_HEADS, HEAD_DIM, HEAD_DIM), F32)
    zero_n = jnp.zeros((B, ML_HEADS, HEAD_DIM), F32)
    zero_m = jnp.zeros((B, ML_HEADS), F32)
    topk_p = min(TOPK_MAX, S // 4)
    topk_s = min(TOPK_MAX, (past + T) // 4)

    mlC_p, mln_p, mlm_p, mlC_s, mln_s, mlm_s, gv_s = [], [], [], [], [], [], []
    ak_p, av_p, ik_p, ak_s, av_s, ik_s = [], [], [], [], [], []
    for l in range(depth):
        if l % 2 == 0:
            e = l // 2
            prm = (norm_even[e], we[e], bif_pad[e], mlstm_hnorm[e], gmlp_ln_g[e], gmlp_ln_b[e], gmlp_w[e], gmlp_b[e],
                   w_out_even_bf[e], mem_qn[l])
            xp, (c, n, m), _ = _even_layer(xp, mem_k_p[l], mem_v_p[l], zero_c, zero_n, zero_m, prm, LANES)
            mlC_p.append(c); mln_p.append(n); mlm_p.append(m)
            xs, (c, n, m), vrows = _even_layer(xs, cmk[l], cmv[l], state_mlstm_C[e], state_mlstm_n[e],
                                               state_mlstm_m[e], prm, T)
            mlC_s.append(c); mln_s.append(n); mlm_s.append(m)
            gv_s.append(vrows[:, :T])
        else:
            o = l // 2
            prm = (norm_odd[o], wo[o], attn_qn[o], attn_kn[o], w_out_odd_bf[o, :AT_W], w_out_odd_bf[o, AT_W:],
                   mem_qn[l])
            x2, tm, h3, pre = _odd_common(xp, mem_k_p[l], mem_v_p[l], prm, tabs_p, True)
            q_bf, k, v, k_bf, vt_bf, qi_bf, ki, ki_bf, wi, yx = pre
            yc = _dsa_prompt(q_bf, qi_bf, wi, h3, k_bf, vt_bf, ki_bf, topk_p)
            xp = _finish_odd(x2, tm, yc.reshape(B * S, AT_W), yx, prm, xp.shape)
            ak_p.append(k.reshape(B, S, AT_KV_HEADS, HEAD_DIM))
            av_p.append(v.reshape(B, S, AT_KV_HEADS, HEAD_DIM))
            ik_p.append(ki)
            x2, tm, h3, pre = _odd_common(xs, cmk[l], cmv[l], prm, tabs_s, False)
            q_bf, k, v, k_bf, v_bf, qi_bf, ki, ki_bf, wi, yx = pre
            q_rows = q_bf[:, :T].reshape(Bd, T * AT_HEADS, HEAD_DIM)
            gc_rows = h3[:, :T, O_GC:O_GC + AT_W].reshape(Bd, T * AT_HEADS, HEAD_DIM)
            qi_st = qi_bf.reshape(Bd, tp, IDX_HEADS, IDX_DIM).transpose(0, 2, 1, 3).reshape(
                Bd, IDX_HEADS * tp, IDX_DIM)
            wi_h = wi[:, :, IDX_DIM:IDX_DIM + IDX_HEADS] * IDX_DIM ** -0.5
            wi_st = jnp.broadcast_to(wi_h.transpose(0, 2, 1).reshape(Bd, IDX_HEADS * tp, 1),
                                     (Bd, IDX_HEADS * tp, LANES))
            kv_page = lambda a: jnp.pad(a.reshape(Bd, tp * AT_KV_HEADS, HEAD_DIM),
                                        ((0, 0), (0, AT_KV_HEADS * (PAGE_SIZE - tp)), (0, 0)))
            ki_page = jnp.pad(ki_bf, ((0, 0), (0, PAGE_SIZE - tp), (0, 0)))
            y_rows = _dsa_sample(q_rows, qi_st, wi_st, gc_rows, kv_page(k), kv_page(v), ki_page,
                                 cache_attn_k, cache_attn_v, cache_idx_k, o, page_table, topk_s, tp, T)
            yc = jnp.pad(y_rows.reshape(Bd, T, AT_W), ((0, 0), (0, tp - T), (0, 0))).reshape(Bd * tp, AT_W)
            xs = _finish_odd(x2, tm, yc, yx, prm, xs.shape)
            ak_s.append(k[:, :T].reshape(Bd, T, AT_KV_HEADS, HEAD_DIM))
            av_s.append(v[:, :T].reshape(Bd, T, AT_KV_HEADS, HEAD_DIM))
            ik_s.append(ki[:, :T])
    return (xp, xs[:, :T],
            jnp.stack(mlC_p), jnp.stack(mln_p), jnp.stack(mlm_p),
            jnp.stack(mlC_s), jnp.stack(mln_s), jnp.stack(mlm_s),
            jnp.stack(gv_s),
            jnp.stack(ak_p), jnp.stack(av_p), jnp.stack(ik_p),
            jnp.stack(ak_s), jnp.stack(av_s), jnp.stack(ik_s),
            mem_k_p.reshape(depth, B, n_mem, XA_HEADS, HEAD_DIM),
            mem_v_p.reshape(depth, B, n_mem, XA_HEADS, HEAD_DIM))
```

```python
import functools
import math

import jax
import jax.numpy as jnp
from jax import lax
from jax.experimental import pallas as pl
from jax.experimental.pallas import tpu as pltpu

F32 = jnp.float32
BF16 = jnp.bfloat16
I32 = jnp.int32

HEAD_DIM = 128
ML_HEADS = 4
GM_GROUPS = 4
AT_HEADS = 8
AT_KV_HEADS = 2
AT_GROUP = AT_HEADS // AT_KV_HEADS
IDX_HEADS = 8
IDX_DIM = 64
XA_HEADS = 4
TOPK_MAX = 256
PAGE_SIZE = 128
ROPE_THETA = 500000.0
ROPE_FRAC = 4
NORM_EPS = 1e-6

ML_W = ML_HEADS * HEAD_DIM
GM_W = GM_GROUPS * HEAD_DIM
XA_W = XA_HEADS * HEAD_DIM
AT_W = AT_HEADS * HEAD_DIM
KV_W = AT_KV_HEADS * HEAD_DIM
IDX_W = IDX_HEADS * IDX_DIM

LANES = 128
SUBLANES = 8
VMEM_LIMIT = 56 * 1024 * 1024
NEG = -1e30
NO_SCORE = -3.0e38
BISECT_STEPS = 16
SAMPLE_PAD = SUBLANES
BF16_SUBLANES = 16
VT_ROWS = HEAD_DIM + BF16_SUBLANES
Q_SCALE = HEAD_DIM ** -0.5 * math.log2(math.e)
PAGES_PER_STEP = 16
COUNT_ACC_TILES = 4
GROUP_SHIFT = AT_GROUP.bit_length() - 1
assert AT_GROUP == 1 << GROUP_SHIFT and AT_KV_HEADS & (AT_KV_HEADS - 1) == 0

E_Q, E_K, E_V, E_O, E_GA, E_U, E_VB, E_GB, E_XQ, E_GX, E_IF = (
    0, 512, 1024, 1536, 2048, 2560, 3072, 3584, 4096, 4608, 5120)
E_MAIN = 5120
E_COLS = E_MAIN + LANES
O_Q, O_GC, O_K, O_V, O_QI, O_XQ, O_GX, O_KI = 0, 1024, 2048, 2304, 2560, 3072, 3584, 4096
O_MAIN = 4096
O_COLS = O_MAIN + LANES


def _cparams(sem):
    return pltpu.CompilerParams(dimension_semantics=sem, vmem_limit_bytes=VMEM_LIMIT)


def _sigmoid(x):
    return 1.0 / (1.0 + jnp.exp(-x))


def _silu(x):
    return x * _sigmoid(x)


def _rms_rows(x, g):
    return x * lax.rsqrt(jnp.mean(x * x, axis=-1, keepdims=True) + NORM_EPS) * g


def _dot_nt(a, b):
    return lax.dot_general(a, b, (((1,), (1,)), ((), ())), preferred_element_type=F32)


def _dot_tn(a, b):
    return lax.dot_general(a, b, (((0,), (0,)), ((), ())), preferred_element_type=F32)


def _dot(a, b):
    return jnp.dot(a, b, preferred_element_type=F32)


def _in_proj_kernel(x_ref, g_ref, w_ref, o_ref, *, col_chunk):
    xn = _rms_rows(x_ref[...], g_ref[...]).astype(BF16)
    n = o_ref.shape[-1]
    for c0 in range(0, n, col_chunk):
        c1 = min(c0 + col_chunk, n)
        o_ref[:, c0:c1] = _dot(xn, w_ref[:, c0:c1])


def _in_proj(x2, g, w_bf, tm):
    m, d = x2.shape
    n = w_bf.shape[1]
    return pl.pallas_call(
        functools.partial(_in_proj_kernel, col_chunk=512),
        grid=(m // tm,),
        in_specs=[pl.BlockSpec((tm, d), lambda i: (i, 0)),
                  pl.BlockSpec((1, d), lambda i: (0, 0)),
                  pl.BlockSpec((d, n), lambda i: (0, 0))],
        out_specs=pl.BlockSpec((tm, n), lambda i: (i, 0)),
        out_shape=jax.ShapeDtypeStruct((m, n), F32),
        compiler_params=_cparams(("parallel",)),
        name="in_proj",
    )(x2, g.reshape(1, d), w_bf)


def _out_proj_kernel(*refs, n_in):
    x_ref = refs[0]
    y_refs = refs[1:1 + n_in]
    w_refs = refs[1 + n_in:1 + 2 * n_in]
    o_ref = refs[1 + 2 * n_in]
    acc = x_ref[...]
    for y_ref, w_ref in zip(y_refs, w_refs):
        acc = acc + _dot(y_ref[...].astype(BF16), w_ref[...])
    o_ref[...] = acc


def _out_proj(x2, ys, ws, tm):
    m, d = x2.shape
    n_in = len(ys)
    in_specs = [pl.BlockSpec((tm, d), lambda i: (i, 0))]
    in_specs += [pl.BlockSpec((tm, y.shape[1]), lambda i: (i, 0)) for y in ys]
    in_specs += [pl.BlockSpec(w.shape, lambda i: (0, 0)) for w in ws]
    return pl.pallas_call(
        functools.partial(_out_proj_kernel, n_in=n_in),
        grid=(m // tm,),
        in_specs=in_specs,
        out_specs=pl.BlockSpec((tm, d), lambda i: (i, 0)),
        out_shape=jax.ShapeDtypeStruct((m, d), F32),
        compiler_params=_cparams(("parallel",)),
        name="out_proj",
    )(x2, *ys, *ws)


def _mem_kv_kernel(mem_ref, w_ref, kn_ref, k_ref, v_ref):
    kv = _dot(mem_ref[...].astype(BF16), w_ref[...])
    kn = kn_ref[...]
    for h in range(XA_HEADS):
        sl = slice(h * HEAD_DIM, (h + 1) * HEAD_DIM)
        k_ref[:, sl] = _rms_rows(kv[:, sl], kn)
    v_ref[...] = kv[:, XA_W:]


def _mem_kv(mem2, w_kv_bf, mem_kn, tm):
    m, d = mem2.shape
    depth = w_kv_bf.shape[0]
    out = jax.ShapeDtypeStruct((depth, m, XA_W), F32)
    return pl.pallas_call(
        _mem_kv_kernel,
        grid=(depth, m // tm),
        in_specs=[pl.BlockSpec((tm, d), lambda l, i: (i, 0)),
                  pl.BlockSpec((None, d, 2 * XA_W), lambda l, i: (l, 0, 0)),
                  pl.BlockSpec((None, 1, HEAD_DIM), lambda l, i: (l, 0, 0))],
        out_specs=[pl.BlockSpec((None, tm, XA_W), lambda l, i: (l, i, 0)),
                   pl.BlockSpec((None, tm, XA_W), lambda l, i: (l, i, 0))],
        out_shape=[out, out],
        compiler_params=_cparams(("parallel", "parallel")),
        name="mem_kv",
    )(mem2, w_kv_bf, mem_kn.reshape(depth, 1, HEAD_DIM))


def _mem_attn_heads(xq, gx, mk_ref, mv_ref, qn):
    outs = []
    for h in range(XA_HEADS):
        sl = slice(h * HEAD_DIM, (h + 1) * HEAD_DIM)
        q = _rms_rows(xq[:, sl], qn).astype(BF16)
        s = _dot_nt(q, mk_ref[:, sl].astype(BF16)) * HEAD_DIM ** -0.5
        s = s - jnp.max(s, axis=-1, keepdims=True)
        p = jnp.exp(s)
        p = p / jnp.sum(p, axis=-1, keepdims=True)
        o = _dot(p.astype(BF16), mv_ref[:, sl].astype(BF16))
        outs.append(o * _silu(gx[:, sl]))
    return outs


def _even_mix_kernel(h_ref, if_ref, c0_ref, n0_ref, m0_ref, bif_ref, hg_ref, lng_ref, lnb_ref,
                     ws_ref, bs_ref, mk_ref, mv_ref, qn_ref,
                     y_ref, vbn_ref, c_ref, n_ref, m_ref, *, t_valid):
    L = h_ref.shape[0]
    ci = pl.program_id(1)

    @pl.when(ci == 0)
    def _():
        c_ref[...] = c0_ref[...]
        n_ref[...] = n0_ref[...]
        m_ref[...] = m0_ref[...]

    row = lax.broadcasted_iota(I32, (L, L), 0)
    col = lax.broadcasted_iota(I32, (L, L), 1)
    eye = row == col
    row_le_col = row <= col
    col_le_row = col <= row
    tcol = lax.broadcasted_iota(I32, (L, 1), 0)

    gates = if_ref[...] + bif_ref[...]
    log_f_all = jnp.minimum(gates, 0.0) - jnp.log(1.0 + jnp.exp(-jnp.abs(gates)))

    for h in range(ML_HEADS):
        sl = slice(h * HEAD_DIM, (h + 1) * HEAD_DIM)
        li_col = gates[:, h:h + 1]
        lf_col = log_f_all[:, ML_HEADS + h:ML_HEADS + h + 1]
        if t_valid < L:
            li_col = jnp.where(tcol < t_valid, li_col, NEG)
            lf_col = jnp.where(tcol < t_valid, lf_col, 0.0)
        li_b = jnp.broadcast_to(li_col, (L, L))
        lf_b = jnp.broadcast_to(lf_col, (L, L))
        li_row = jnp.sum(jnp.where(eye, li_b, 0.0), axis=0, keepdims=True)
        lf_row = jnp.sum(jnp.where(eye, lf_b, 0.0), axis=0, keepdims=True)
        b_row = jnp.sum(jnp.where(row_le_col, lf_b, 0.0), axis=0, keepdims=True)
        b_col = jnp.sum(jnp.where(col_le_row, jnp.broadcast_to(lf_row, (L, L)), 0.0),
                        axis=1, keepdims=True)
        m_prev = m_ref[h:h + 1, 0:1]
        n_prev = n_ref[h:h + 1, :]
        c_prev = c_ref[h]

        dlog = jnp.where(col_le_row, b_col - b_row + li_row, NEG)
        inter = b_col + m_prev
        m_t = jnp.maximum(inter, jnp.max(dlog, axis=1, keepdims=True))
        w_intra = jnp.exp(dlog - m_t)
        w_inter = jnp.exp(inter - m_t)
        q = h_ref[:, E_Q + h * HEAD_DIM:E_Q + (h + 1) * HEAD_DIM]
        k = h_ref[:, E_K + h * HEAD_DIM:E_K + (h + 1) * HEAD_DIM] * HEAD_DIM ** -0.5
        v = h_ref[:, E_V + h * HEAD_DIM:E_V + (h + 1) * HEAD_DIM]
        q_bf = q.astype(BF16)
        k_bf = k.astype(BF16)
        qk = _dot_nt(q_bf, k_bf) * w_intra
        num = _dot(qk.astype(BF16), v.astype(BF16)) + w_inter * _dot(q_bf, c_prev.astype(BF16))
        den = jnp.sum(qk, axis=1, keepdims=True) + w_inter * jnp.sum(q * n_prev, axis=1, keepdims=True)
        hh = num / jnp.maximum(jnp.abs(den), jnp.exp(-m_t))

        b_end = b_col[L - 1:L, :]
        g = b_end - b_col + li_col
        m_new = jnp.maximum(b_end + m_prev, jnp.max(g, axis=0, keepdims=True))
        a_prev = jnp.exp(b_end + m_prev - m_new)
        a_tok = jnp.exp(g - m_new)
        c_ref[h] = a_prev * c_prev + _dot_tn(k_bf, (a_tok * v).astype(BF16))
        n_ref[h:h + 1, :] = a_prev * n_prev + jnp.sum(a_tok * k, axis=0, keepdims=True)
        m_ref[h:h + 1, :] = jnp.broadcast_to(m_new, (1, LANES))

        ha = _rms_rows(hh, hg_ref[:, sl])
        y_ref[:, sl] = ha * _sigmoid(h_ref[:, E_O + h * HEAD_DIM:E_O + (h + 1) * HEAD_DIM]) * _silu(
            h_ref[:, E_GA + h * HEAD_DIM:E_GA + (h + 1) * HEAD_DIM])

    vb = h_ref[:, E_VB:E_VB + GM_W]
    vc = vb - jnp.mean(vb, axis=-1, keepdims=True)
    var = jnp.mean(vc * vc, axis=-1, keepdims=True)
    vbn = vc * lax.rsqrt(var + NORM_EPS) * lng_ref[...] + lnb_ref[...]
    vbn_ref[...] = vbn
    for g_i in range(GM_GROUPS):
        sl = slice(g_i * HEAD_DIM, (g_i + 1) * HEAD_DIM)
        w = jnp.where(col_le_row, ws_ref[g_i], 0.0).astype(BF16)
        mixed = _dot(w, vbn[:, sl].astype(BF16)) + bs_ref[:, g_i:g_i + 1]
        y_ref[:, ML_W + g_i * HEAD_DIM:ML_W + (g_i + 1) * HEAD_DIM] = (
            h_ref[:, E_U + g_i * HEAD_DIM:E_U + (g_i + 1) * HEAD_DIM] * mixed
            * _silu(h_ref[:, E_GB + g_i * HEAD_DIM:E_GB + (g_i + 1) * HEAD_DIM]))

    outs = _mem_attn_heads(h_ref[:, E_XQ:E_XQ + XA_W], h_ref[:, E_GX:E_GX + XA_W], mk_ref, mv_ref, qn_ref[...])
    for h, o in enumerate(outs):
        y_ref[:, ML_W + GM_W + h * HEAD_DIM:ML_W + GM_W + (h + 1) * HEAD_DIM] = o


def _even_mix(h3, c0, n0, m0b, bif, hg, lng, lnb, ws, bs_t, mk, mv, qn, L, t_valid):
    b, t, _ = h3.shape
    nc = t // L
    n_mem = mk.shape[1]
    full2 = lambda bi, ci: (0, 0)
    return pl.pallas_call(
        functools.partial(_even_mix_kernel, t_valid=t_valid),
        grid=(b, nc),
        in_specs=[pl.BlockSpec((None, L, E_MAIN), lambda bi, ci: (bi, ci, 0)),
                  pl.BlockSpec((None, L, LANES), lambda bi, ci: (bi, ci, E_MAIN // LANES)),
                  pl.BlockSpec((None, ML_HEADS, HEAD_DIM, HEAD_DIM), lambda bi, ci: (bi, 0, 0, 0)),
                  pl.BlockSpec((None, ML_HEADS, HEAD_DIM), lambda bi, ci: (bi, 0, 0)),
                  pl.BlockSpec((None, ML_HEADS, LANES), lambda bi, ci: (bi, 0, 0)),
                  pl.BlockSpec((1, LANES), full2),
                  pl.BlockSpec((1, ML_W), full2),
                  pl.BlockSpec((1, GM_W), full2),
                  pl.BlockSpec((1, GM_W), full2),
                  pl.BlockSpec((GM_GROUPS, L, L), lambda bi, ci: (0, 0, 0)),
                  pl.BlockSpec((L, LANES), full2),
                  pl.BlockSpec((None, n_mem, XA_W), lambda bi, ci: (bi, 0, 0)),
                  pl.BlockSpec((None, n_mem, XA_W), lambda bi, ci: (bi, 0, 0)),
                  pl.BlockSpec((1, HEAD_DIM), full2)],
        out_specs=[pl.BlockSpec((None, L, ML_W + GM_W + XA_W), lambda bi, ci: (bi, ci, 0)),
                   pl.BlockSpec((None, L, GM_W), lambda bi, ci: (bi, ci, 0)),
                   pl.BlockSpec((None, ML_HEADS, HEAD_DIM, HEAD_DIM), lambda bi, ci: (bi, 0, 0, 0)),
                   pl.BlockSpec((None, ML_HEADS, HEAD_DIM), lambda bi, ci: (bi, 0, 0)),
                   pl.BlockSpec((None, ML_HEADS, LANES), lambda bi, ci: (bi, 0, 0))],
        out_shape=[jax.ShapeDtypeStruct((b, t, ML_W + GM_W + XA_W), F32),
                   jax.ShapeDtypeStruct((b, t, GM_W), F32),
                   jax.ShapeDtypeStruct((b, ML_HEADS, HEAD_DIM, HEAD_DIM), F32),
                   jax.ShapeDtypeStruct((b, ML_HEADS, HEAD_DIM), F32),
                   jax.ShapeDtypeStruct((b, ML_HEADS, LANES), F32)],
        compiler_params=_cparams(("parallel", "arbitrary")),
        name="even_mix",
    )(h3, h3, c0, n0, m0b, bif, hg, lng, lnb, ws, bs_t, mk, mv, qn)


def _rope_lanes(x, cos, sin_lo, sin_hi, half):
    n = x.shape[-1]
    return x * cos + pltpu.roll(x, n - half, 1) * sin_lo + pltpu.roll(x, half, 1) * sin_hi


def _odd_pre_kernel(h_ref, kiwi_ref, ca_ref, sa1_ref, sa2_ref, ci_ref, si1_ref, si2_ref,
                    qn_ref, kn_ref, mk_ref, mv_ref, mqn_ref,
                    q_ref, k_ref, v_ref, kbf_ref, vbf_ref, qi_ref, ki_ref, kibf_ref, wi_ref, yx_ref, *, v_transposed):
    ca, sa1, sa2 = ca_ref[...], sa1_ref[...], sa2_ref[...]
    ci, si1, si2 = ci_ref[...], si1_ref[...], si2_ref[...]
    a_half = HEAD_DIM // ROPE_FRAC // 2
    i_half = IDX_DIM // ROPE_FRAC // 2
    for h in range(AT_HEADS):
        sl = slice(O_Q + h * HEAD_DIM, O_Q + (h + 1) * HEAD_DIM)
        qh = _rope_lanes(_rms_rows(h_ref[:, sl], qn_ref[...]), ca, sa1, sa2, a_half)
        q_ref[:, h * HEAD_DIM:(h + 1) * HEAD_DIM] = (qh * Q_SCALE).astype(BF16)
    for h in range(AT_KV_HEADS):
        sl = slice(O_K + h * HEAD_DIM, O_K + (h + 1) * HEAD_DIM)
        kh = _rope_lanes(_rms_rows(h_ref[:, sl], kn_ref[...]), ca, sa1, sa2, a_half)
        k_ref[:, h * HEAD_DIM:(h + 1) * HEAD_DIM] = kh
        kbf_ref[:, h * HEAD_DIM:(h + 1) * HEAD_DIM] = kh.astype(BF16)
    v = h_ref[:, O_V:O_V + KV_W]
    v_ref[...] = v
    if v_transposed:
        tail_row = lax.broadcasted_iota(I32, (VT_ROWS - HEAD_DIM, v.shape[0]), 0)
        tail = jnp.where(tail_row == 0, 1.0, 0.0).astype(BF16)
        for n in range(AT_KV_HEADS):
            vbf_ref[n, :HEAD_DIM, :] = v[:, n * HEAD_DIM:(n + 1) * HEAD_DIM].T.astype(BF16)
            vbf_ref[n, HEAD_DIM:, :] = tail
    else:
        vbf_ref[...] = v.astype(BF16)
    for j in range(IDX_W // LANES):
        sl = slice(O_QI + j * LANES, O_QI + (j + 1) * LANES)
        qi_ref[:, j * LANES:(j + 1) * LANES] = _rope_lanes(h_ref[:, sl], ci, si1, si2, i_half).astype(BF16)
    kiwi = kiwi_ref[...]
    ki = _rope_lanes(kiwi, ci, si1, si2, i_half)[:, :IDX_DIM]
    ki_ref[...] = ki
    kibf_ref[...] = ki.astype(BF16)
    lane = lax.broadcasted_iota(I32, kiwi.shape, 1)
    wsel = (lane >= IDX_DIM) & (lane < IDX_DIM + IDX_HEADS)
    wi_ref[...] = jnp.where(wsel, kiwi * IDX_HEADS ** -0.5, 0.0)
    outs = _mem_attn_heads(h_ref[:, O_XQ:O_XQ + XA_W], h_ref[:, O_GX:O_GX + XA_W], mk_ref, mv_ref, mqn_ref[...])
    for h, o in enumerate(outs):
        yx_ref[:, h * HEAD_DIM:(h + 1) * HEAD_DIM] = o


def _odd_pre(h3, tabs, qn, kn, mk, mv, mqn, tq, v_transposed):
    b, t, _ = h3.shape
    n_mem = mk.shape[1]
    row_blk = lambda w: pl.BlockSpec((None, tq, w), lambda bi, ti: (bi, ti, 0))
    tab = pl.BlockSpec((tq, LANES), lambda bi, ti: (ti, 0))
    vec = pl.BlockSpec((1, HEAD_DIM), lambda bi, ti: (0, 0))
    mem = pl.BlockSpec((None, n_mem, XA_W), lambda bi, ti: (bi, 0, 0))
    sds = lambda w, dt: jax.ShapeDtypeStruct((b, t, w), dt)
    if v_transposed:
        vbf_spec = pl.BlockSpec((None, AT_KV_HEADS, VT_ROWS, tq), lambda bi, ti: (bi, 0, 0, ti))
        vbf_shape = jax.ShapeDtypeStruct((b, AT_KV_HEADS, VT_ROWS, t), BF16)
    else:
        vbf_spec, vbf_shape = row_blk(KV_W), sds(KV_W, BF16)
    return pl.pallas_call(
        functools.partial(_odd_pre_kernel, v_transposed=v_transposed),
        grid=(b, t // tq),
        in_specs=[row_blk(O_MAIN),
                  pl.BlockSpec((None, tq, LANES), lambda bi, ti: (bi, ti, O_MAIN // LANES)),
                  tab, tab, tab, tab, tab, tab, vec, vec, mem, mem, vec],
        out_specs=[row_blk(AT_W), row_blk(KV_W), row_blk(KV_W), row_blk(KV_W), vbf_spec,
                   row_blk(IDX_W), row_blk(IDX_DIM), row_blk(IDX_DIM), row_blk(LANES), row_blk(XA_W)],
        out_shape=[sds(AT_W, BF16), sds(KV_W, F32), sds(KV_W, F32), sds(KV_W, BF16), vbf_shape,
                   sds(IDX_W, BF16), sds(IDX_DIM, F32), sds(IDX_DIM, BF16), sds(LANES, F32), sds(XA_W, F32)],
        compiler_params=_cparams(("parallel", "parallel")),
        name="odd_pre",
    )(h3, h3, *tabs, qn, kn, mk, mv, mqn)


def _rope_tables(pos):
    pos = pos.astype(F32)[:, None]

    def build(width, reps):
        rd = width // ROPE_FRAC
        half = rd // 2
        inv = ROPE_THETA ** (-jnp.arange(half, dtype=F32) / half)
        ang = pos * inv[None, :]
        cos, sin = jnp.cos(ang), jnp.sin(ang)
        ones = jnp.ones((pos.shape[0], width - rd), F32)
        zeros_h = jnp.zeros((pos.shape[0], half), F32)
        zeros_r = jnp.zeros((pos.shape[0], width - rd), F32)
        c = jnp.concatenate([cos, cos, ones], axis=1)
        s_lo = jnp.concatenate([-sin, zeros_h, zeros_r], axis=1)
        s_hi = jnp.concatenate([zeros_h, sin, zeros_r], axis=1)
        return [jnp.tile(a, (1, reps)) for a in (c, s_lo, s_hi)]

    return build(HEAD_DIM, 1) + build(IDX_DIM, LANES // IDX_DIM)


def _topk_threshold(score_ref, idx, n_chunks, chunk_shape, key_axis, k, pos_bits, stats=None):
    red_shape = tuple(1 if a == key_axis else d for a, d in enumerate(chunk_shape))
    pos_in_chunk = lax.broadcasted_iota(I32, chunk_shape, key_axis)
    acc_rows = min(chunk_shape[0], COUNT_ACC_TILES * SUBLANES) if key_axis == 0 else chunk_shape[0]
    acc_shape = (acc_rows, chunk_shape[1])
    zeros = jnp.zeros(acc_shape, F32)

    def folded(x, op=jnp.add):
        parts = [x[r:r + acc_rows] for r in range(0, chunk_shape[0], acc_rows)]
        while len(parts) > 1:
            parts = [op(a, b) for a, b in zip(parts[::2], parts[1::2])] + ([parts[-1]] if len(parts) % 2 else [])
        return parts[0]

    def count(pred):
        def body(c, acc):
            return acc + folded(jnp.where(pred(score_ref[idx(c)], c), 1.0, 0.0))

        return jnp.sum(lax.fori_loop(0, n_chunks, body, zeros), axis=key_axis, keepdims=True)

    def count_ge(t):
        t_b = jnp.broadcast_to(t, chunk_shape)
        return count(lambda s, c: s >= t_b)

    def range_body(c, carry):
        top, bottom, valid = carry
        s = score_ref[idx(c)]
        ok = s > NO_SCORE
        return (jnp.maximum(top, folded(s, jnp.maximum)),
                jnp.minimum(bottom, folded(jnp.where(ok, s, -NO_SCORE), jnp.minimum)),
                valid + folded(jnp.where(ok, 1.0, 0.0)))

    if stats is None:
        stats = lax.fori_loop(0, n_chunks, range_body,
                              (jnp.full(acc_shape, NO_SCORE, F32), jnp.full(acc_shape, -NO_SCORE, F32), zeros))
    top, bottom, valid = stats
    top = jnp.max(top, axis=key_axis, keepdims=True)
    bottom = jnp.min(bottom, axis=key_axis, keepdims=True)
    rank = jnp.minimum(jnp.sum(valid, axis=key_axis, keepdims=True), k)

    def bisect_step(i, carry):
        lo, mid_hi, hi = carry
        mid = 0.5 * (lo + mid_hi)
        enough = count_ge(mid) >= rank
        return jnp.where(enough, mid, lo), jnp.where(enough, mid_hi, mid), jnp.where(enough, hi, mid)

    _, _, hi = lax.fori_loop(0, BISECT_STEPS, bisect_step, (bottom, top, jnp.full(red_shape, -NO_SCORE, F32)))

    def walk_cond(carry):
        return jnp.max(carry[2]) > 0.0

    def walk_body(carry):
        hi, thr, todo = carry
        hi_b = jnp.broadcast_to(hi, chunk_shape)

        def body(c, acc):
            s = score_ref[idx(c)]
            return jnp.maximum(acc, folded(jnp.where(s < hi_b, s, NO_SCORE), jnp.maximum))

        cand = jnp.max(lax.fori_loop(0, n_chunks, body, jnp.full(acc_shape, NO_SCORE, F32)),
                       axis=key_axis, keepdims=True)
        found = count_ge(cand) >= rank
        open_ = todo > 0.0
        return (jnp.where(open_, cand, hi), jnp.where(open_, cand, thr),
                jnp.where(open_ & jnp.logical_not(found), 1.0, 0.0))

    _, thr, _ = lax.while_loop(walk_cond, walk_body, (hi, bottom, jnp.ones(red_shape, F32)))
    thr_b = jnp.broadcast_to(thr, chunk_shape)

    def count_body(c, carry):
        gt, eq = carry
        s = score_ref[idx(c)]
        return (gt + folded(jnp.where(s > thr_b, 1.0, 0.0)), eq + folded(jnp.where(s == thr_b, 1.0, 0.0)))

    gt, eq = lax.fori_loop(0, n_chunks, count_body, (zeros, zeros))
    need = rank - jnp.sum(gt, axis=key_axis, keepdims=True)
    excess = jnp.sum(eq, axis=key_axis, keepdims=True) - need

    def pos_step(i, cut):
        cand = cut + lax.shift_left(jnp.int32(1), pos_bits - 1 - i)
        cand_b = jnp.broadcast_to(cand, chunk_shape)
        cnt = count(lambda s, c: (s == thr_b) & (c * chunk_shape[key_axis] + pos_in_chunk < cand_b))
        return jnp.where(cnt < need, cand, cut)

    any_excess = jnp.max(excess) > 0.0
    cut = lax.cond(any_excess,
                   lambda: lax.fori_loop(0, pos_bits, pos_step, jnp.zeros(red_shape, I32)),
                   lambda: jnp.full(red_shape, 2 ** pos_bits, I32))
    return thr, cut, any_excess


def _selected(scores, thr, cut, pos):
    return (scores > thr) | ((scores == thr) & (pos <= cut))


def _dsa_prompt_kernel(q_ref, qi_ref, wi_ref, gc_ref, k_ref, vt_ref, ki_ref, o_ref,
                       sc_ref, qt_ref, qit_ref, wit_ref, m_ref, l_ref, acc_ref, *, topk, pos_bits, ck, ca):
    qb = pl.program_id(1)
    nq = q_ref.shape[0]
    n_chunks = (qb * nq) // ck + 1
    chunk = (ck, LANES)
    stat_rows = COUNT_ACC_TILES * SUBLANES
    kpos_in = lax.broadcasted_iota(I32, chunk, 0)
    qpos = qb * nq + lax.broadcasted_iota(I32, chunk, 1)

    for h in range(AT_HEADS):
        sl = slice(h * HEAD_DIM, (h + 1) * HEAD_DIM)
        qt_ref[:, sl] = q_ref[:, sl].astype(F32).T.astype(BF16)
    for j in range(IDX_W // LANES):
        t = qi_ref[:, j * LANES:(j + 1) * LANES].astype(F32).T.astype(BF16)
        for i in range(LANES // IDX_DIM):
            h = j * (LANES // IDX_DIM) + i
            qit_ref[:, h * LANES:(h + 1) * LANES] = t[i * IDX_DIM:(i + 1) * IDX_DIM, :]
    wit_ref[...] = wi_ref[...].T[IDX_DIM:IDX_DIM + IDX_HEADS, :] * IDX_DIM ** -0.5

    def score_body(c, carry):
        off = pl.multiple_of(c * ck, ck)
        sc = _dot(ki_ref[pl.ds(off, ck), :], qit_ref[...])
        acc = jnp.zeros(chunk, F32)
        for h in range(IDX_HEADS):
            acc = acc + jnp.maximum(sc[:, h * LANES:(h + 1) * LANES], 0.0) * wit_ref[h:h + 1, :]
        ok = off + kpos_in <= qpos
        sc_ref[pl.ds(off, ck), :] = jnp.where(ok, acc, NO_SCORE)
        top, bottom, valid = carry

        def fold(x, op):
            parts = [x[r:r + stat_rows] for r in range(0, ck, stat_rows)]
            while len(parts) > 1:
                parts = [op(a, b) for a, b in zip(parts[::2], parts[1::2])]
            return parts[0]

        return (jnp.maximum(top, fold(jnp.where(ok, acc, NO_SCORE), jnp.maximum)),
                jnp.minimum(bottom, fold(jnp.where(ok, acc, -NO_SCORE), jnp.minimum)),
                valid + fold(jnp.where(ok, 1.0, 0.0), jnp.add))

    stats = lax.fori_loop(0, n_chunks, score_body,
                          (jnp.full((stat_rows, LANES), NO_SCORE, F32), jnp.full((stat_rows, LANES), -NO_SCORE, F32),
                           jnp.zeros((stat_rows, LANES), F32)))

    n_att = n_chunks * ck // ca

    def idx(c):
        return (pl.ds(pl.multiple_of(c * ca, ca), ca), slice(None))

    thr, cut, any_excess = _topk_threshold(sc_ref, idx, n_att, (ca, LANES), 0, float(topk), pos_bits, stats)

    @pl.when(any_excess)
    def _():
        def demote_body(c, carry):
            off = pl.multiple_of(c * ck, ck)
            sc = sc_ref[pl.ds(off, ck), :]
            sc_ref[pl.ds(off, ck), :] = jnp.where((sc == thr) & (off + kpos_in > cut), NO_SCORE, sc)
            return carry

        lax.fori_loop(0, n_chunks, demote_body, 0)

    thr_eff = jnp.broadcast_to(thr, (ca, LANES))

    m_ref[...] = jnp.full(m_ref.shape, NEG, F32)
    l_ref[...] = jnp.zeros(l_ref.shape, F32)
    acc_ref[...] = jnp.zeros(acc_ref.shape, F32)
    pair = 2 * HEAD_DIM
    n_pairs = AT_HEADS // 2

    def attn_body(c, carry):
        off = pl.multiple_of(c * ca, ca)
        bias = jnp.where(sc_ref[pl.ds(off, ca), :] >= thr_eff, 0.0, NEG)
        bias2 = jnp.concatenate([bias, bias], axis=1)
        scores = [_dot(k_ref[pl.ds(off, ca), ((2 * hp) // AT_GROUP) * HEAD_DIM:((2 * hp) // AT_GROUP + 1) * HEAD_DIM],
                       qt_ref[:, hp * pair:(hp + 1) * pair]) + bias2 for hp in range(n_pairs)]
        for hp, s in enumerate(scores):
            n = (2 * hp) // AT_GROUP
            sl = slice(hp * pair, (hp + 1) * pair)
            m_old = m_ref[:, sl]
            m_new = jnp.maximum(m_old, jnp.max(s, axis=0, keepdims=True))
            p = jnp.exp2(s - m_new).astype(BF16)
            alpha = jnp.exp2(m_old - m_new)
            m_ref[:, sl] = m_new
            pv = _dot(vt_ref[n, :, pl.ds(off, ca)], p)
            acc_ref[:, sl] = alpha * acc_ref[:, sl] + pv[:HEAD_DIM]
            l_ref[:, sl] = alpha * l_ref[:, sl] + pv[HEAD_DIM:HEAD_DIM + 1]
        return carry

    lax.fori_loop(0, n_att, attn_body, 0)
    for h in range(AT_HEADS):
        sl = slice(h * HEAD_DIM, (h + 1) * HEAD_DIM)
        o_ref[:, sl] = (acc_ref[:, sl] / l_ref[:, sl]).T * _silu(gc_ref[:, sl])


def _dsa_prompt(q_bf, qi_bf, wi, h3, k_bf, vt_bf, ki_bf, topk):
    b, s, _ = q_bf.shape
    nq = LANES
    ca = math.gcd(s, 4 * LANES)
    ck = ca
    assert s % ca == 0
    pos_bits = max(1, int(math.ceil(math.log2(s))))
    qblk = lambda w: pl.BlockSpec((None, nq, w), lambda bi, qi: (bi, qi, 0))
    full = lambda w: pl.BlockSpec((None, s, w), lambda bi, qi: (bi, 0, 0))
    return pl.pallas_call(
        functools.partial(_dsa_prompt_kernel, topk=topk, pos_bits=pos_bits, ck=ck, ca=ca),
        grid=(b, s // nq),
        in_specs=[qblk(AT_W), qblk(IDX_W), qblk(LANES),
                  pl.BlockSpec((None, nq, AT_W), lambda bi, qi: (bi, qi, O_GC // AT_W)),
                  full(KV_W),
                  pl.BlockSpec((None, AT_KV_HEADS, VT_ROWS, s), lambda bi, qi: (bi, 0, 0, 0)),
                  full(IDX_DIM)],
        out_specs=qblk(AT_W),
        out_shape=jax.ShapeDtypeStruct((b, s, AT_W), F32),
        scratch_shapes=[pltpu.VMEM((s, LANES), F32),
                        pltpu.VMEM((HEAD_DIM, AT_HEADS * nq), BF16),
                        pltpu.VMEM((IDX_DIM, IDX_HEADS * nq), BF16),
                        pltpu.VMEM((IDX_HEADS, nq), F32),
                        pltpu.VMEM((1, AT_HEADS * nq), F32),
                        pltpu.VMEM((1, AT_HEADS * nq), F32),
                        pltpu.VMEM((HEAD_DIM, AT_HEADS * nq), F32)],
        compiler_params=_cparams(("parallel", "arbitrary")),
        name="dsa_prompt",
    )(q_bf, qi_bf, wi, h3, k_bf, vt_bf, ki_bf)


def _dsa_sample_score_kernel(pt_ref, qi_ref, wi_ref, kin_ref, *rest, n_pp, n_steps, t_pad):
    page_refs = rest[:n_pp]
    sc_ref = rest[n_pp]
    step = pl.program_id(1)
    qi = qi_ref[...]
    wi = wi_ref[...]

    def page_scores(ki_bf):
        sc = jnp.maximum(_dot_nt(qi, ki_bf), 0.0) * wi
        acc = jnp.zeros((t_pad, LANES), F32)
        for h in range(IDX_HEADS):
            acc = acc + sc[h * t_pad:(h + 1) * t_pad, :]
        return acc

    for j in range(n_pp):
        off = pl.multiple_of((step * n_pp + j) * PAGE_SIZE, PAGE_SIZE)
        sc_ref[:, pl.ds(off, PAGE_SIZE)] = page_scores(page_refs[j][...].astype(BF16))

    @pl.when(step == n_steps - 1)
    def _():
        past = n_steps * n_pp * PAGE_SIZE
        shape = (t_pad, LANES)
        tok = lax.broadcasted_iota(I32, shape, 0)
        lane = lax.broadcasted_iota(I32, shape, 1)
        sc_ref[:, past:past + LANES] = jnp.where(lane <= tok, page_scores(kin_ref[...]), NO_SCORE)


def _dsa_sample_select_kernel(sc_ref, selx_ref, *, topk, pos_bits):
    rows, width = sc_ref.shape
    n_chunks = width // LANES
    chunk = (rows, LANES)
    xw = AT_KV_HEADS * LANES

    def idx(c):
        return (slice(None), pl.ds(pl.multiple_of(c * LANES, LANES), LANES))

    thr, cut, _ = _topk_threshold(sc_ref, idx, n_chunks, chunk, 1, float(topk), pos_bits)

    lane = lax.broadcasted_iota(I32, chunk, 1)
    src = lax.broadcasted_iota(I32, (LANES, xw), 0)
    dst = lax.broadcasted_iota(I32, (LANES, xw), 1)
    repeat = jnp.where((dst >= src * AT_KV_HEADS) & (dst < (src + 1) * AT_KV_HEADS), 1.0, 0.0).astype(BF16)

    def sel_body(c, carry):
        sel = _selected(sc_ref[idx(c)], thr, cut, c * LANES + lane)
        sel01 = jnp.where(sel, 1.0, 0.0).astype(BF16)
        selx_ref[:, pl.ds(pl.multiple_of(c * xw, xw), xw)] = _dot(sel01, repeat).astype(BF16)
        return carry

    lax.fori_loop(0, n_chunks, sel_body, 0)


def _dsa_sample_attn_kernel(pt_ref, q_ref, selx_ref, selxn_ref, kn_ref, vn_ref, gc_ref, *rest, n_pp, n_steps, n_tok):
    k_refs = rest[:n_pp]
    v_refs = rest[n_pp:2 * n_pp]
    o_ref, m_ref, l_ref, acc_ref = rest[2 * n_pp:2 * n_pp + 4]
    step = pl.program_id(1)
    rows = n_tok * AT_HEADS
    pw = AT_KV_HEADS * PAGE_SIZE
    shape = (rows, LANES)
    row = lax.broadcasted_iota(I32, (rows, pw), 0)
    col = lax.broadcasted_iota(I32, (rows, pw), 1)
    own_head = (lax.shift_right_logical(row, GROUP_SHIFT) & (AT_KV_HEADS - 1)) == (col & (AT_KV_HEADS - 1))
    q = q_ref[...]

    @pl.when(step == 0)
    def _():
        m_ref[...] = jnp.full(m_ref.shape, NEG, F32)
        l_ref[...] = jnp.zeros(l_ref.shape, F32)
        acc_ref[...] = jnp.zeros(acc_ref.shape, F32)

    def attend(blocks):
        parts = []
        for sx, k_ref, _ in blocks:
            sel = jnp.concatenate([jnp.broadcast_to(sx[t:t + 1], (AT_HEADS, pw)) for t in range(n_tok)], axis=0)
            parts.append(_dot_nt(q, k_ref[...].astype(BF16)) + jnp.where((sel > 0.5) & own_head, 0.0, NEG))
        s = jnp.concatenate(parts, axis=1)
        m_old = m_ref[...]
        m_new = jnp.maximum(m_old, jnp.broadcast_to(jnp.max(s, axis=1, keepdims=True), shape))
        p = jnp.exp2(s - m_new[:, 0:1])
        alpha = jnp.exp2(m_old - m_new)
        l_ref[...] = alpha * l_ref[...] + jnp.broadcast_to(jnp.sum(p, axis=1, keepdims=True), shape)
        acc = alpha * acc_ref[...]
        for j, (_, _, v_ref) in enumerate(blocks):
            acc = acc + _dot(p[:, j * pw:(j + 1) * pw].astype(BF16), v_ref[...].astype(BF16))
        acc_ref[...] = acc
        m_ref[...] = m_new

    selx = selx_ref[...].astype(F32)
    attend([(selx[:, j * pw:(j + 1) * pw], k_refs[j], v_refs[j]) for j in range(n_pp)])

    @pl.when(step == n_steps - 1)
    def _():
        attend([(selxn_ref[...].astype(F32), kn_ref, vn_ref)])
        o_ref[...] = acc_ref[...] / l_ref[...] * _silu(gc_ref[...])


def _dsa_sample(q_rows, qi_st, wi_st, gc_rows, k_new, v_new, ki_new, cache_k, cache_v, cache_ik, layer, page_table,
                topk, t_pad, n_tok):
    bd = q_rows.shape[0]
    n_pages = page_table.shape[1]
    n_pp = math.gcd(n_pages, PAGES_PER_STEP)
    n_steps = n_pages // n_pp
    past = n_pages * PAGE_SIZE
    width = past + LANES
    pos_bits = max(1, int(math.ceil(math.log2(width))))
    rows_i = IDX_HEADS * t_pad
    rows_a = AT_HEADS * n_tok
    pw = AT_KV_HEADS * PAGE_SIZE
    ck = cache_k.reshape(cache_k.shape[0], cache_k.shape[1], pw, HEAD_DIM)
    cv = cache_v.reshape(cache_v.shape[0], cache_v.shape[1], pw, HEAD_DIM)

    def page_spec(rows, cols, j):
        return pl.BlockSpec((None, None, rows, cols), lambda bi, si, pt: (layer, pt[bi, si * n_pp + j], 0, 0))

    per_b3 = lambda r, w: pl.BlockSpec((None, r, w), lambda bi, si, pt: (bi, 0, 0))
    scores = pl.pallas_call(
        functools.partial(_dsa_sample_score_kernel, n_pp=n_pp, n_steps=n_steps, t_pad=t_pad),
        grid_spec=pltpu.PrefetchScalarGridSpec(
            num_scalar_prefetch=1,
            grid=(bd, n_steps),
            in_specs=[per_b3(rows_i, IDX_DIM), per_b3(rows_i, LANES), per_b3(PAGE_SIZE, IDX_DIM)]
                     + [page_spec(PAGE_SIZE, IDX_DIM, j) for j in range(n_pp)],
            out_specs=per_b3(t_pad, width)),
        out_shape=jax.ShapeDtypeStruct((bd, t_pad, width), F32),
        compiler_params=_cparams(("parallel", "arbitrary")),
        name="dsa_sample_score",
    )(page_table, qi_st, wi_st, ki_new, *([cache_ik] * n_pp))

    n_q = bd * n_tok
    rb = min(LANES, -(-n_q // SUBLANES) * SUBLANES)
    n_q_pad = -(-n_q // rb) * rb
    scores_q = jnp.pad(scores[:, :n_tok].reshape(n_q, width), ((0, n_q_pad - n_q), (0, 0)),
                       constant_values=NO_SCORE)
    selx = pl.pallas_call(
        functools.partial(_dsa_sample_select_kernel, topk=topk, pos_bits=pos_bits),
        grid=(n_q_pad // rb,),
        in_specs=[pl.BlockSpec((rb, width), lambda i: (i, 0))],
        out_specs=pl.BlockSpec((rb, AT_KV_HEADS * width), lambda i: (i, 0)),
        out_shape=jax.ShapeDtypeStruct((n_q_pad, AT_KV_HEADS * width), BF16),
        compiler_params=_cparams(("parallel",)),
        name="dsa_sample_select",
    )(scores_q)
    selx = selx[:n_q].reshape(bd, n_tok, AT_KV_HEADS * width)

    return pl.pallas_call(
        functools.partial(_dsa_sample_attn_kernel, n_pp=n_pp, n_steps=n_steps, n_tok=n_tok),
        grid_spec=pltpu.PrefetchScalarGridSpec(
            num_scalar_prefetch=1,
            grid=(bd, n_steps),
            in_specs=[per_b3(rows_a, HEAD_DIM),
                      pl.BlockSpec((None, n_tok, n_pp * pw), lambda bi, si, pt: (bi, 0, si)),
                      pl.BlockSpec((None, n_tok, pw), lambda bi, si, pt: (bi, 0, n_pages)),
                      per_b3(pw, HEAD_DIM), per_b3(pw, HEAD_DIM), per_b3(rows_a, HEAD_DIM)]
                     + [page_spec(pw, HEAD_DIM, j) for j in range(n_pp)]
                     + [page_spec(pw, HEAD_DIM, j) for j in range(n_pp)],
            out_specs=per_b3(rows_a, HEAD_DIM),
            scratch_shapes=[pltpu.VMEM((rows_a, LANES), F32),
                            pltpu.VMEM((rows_a, LANES), F32),
                            pltpu.VMEM((rows_a, HEAD_DIM), F32)]),
        out_shape=jax.ShapeDtypeStruct((bd, rows_a, HEAD_DIM), F32),
        compiler_params=_cparams(("parallel", "arbitrary")),
        name="dsa_sample_attn",
    )(page_table, q_rows, selx, selx, k_new, v_new, gc_rows, *([ck] * n_pp), *([cv] * n_pp))


def _row_tile(m):
    return math.gcd(m, 256)


def _even_layer(x3, mk, mv, c0, n0, m0, prm, t_valid):
    norm_g, w_in_bf, bif, hg, lng, lnb, ws, bs, w_out_bf, mqn = prm
    b, t, d = x3.shape
    L = math.gcd(t, LANES)
    x2 = x3.reshape(b * t, d)
    tm = _row_tile(b * t)
    h3 = _in_proj(x2, norm_g, w_in_bf, tm).reshape(b, t, E_COLS)
    m0b = jnp.broadcast_to(m0[:, :, None], (b, ML_HEADS, LANES))
    bs_t = jnp.pad(bs[:, :L].T, ((0, 0), (0, LANES - GM_GROUPS)))
    y, vbn, c, n, mb = _even_mix(h3, c0, n0, m0b, bif, hg.reshape(1, ML_W), lng.reshape(1, GM_W),
                                 lnb.reshape(1, GM_W), ws[:, :L, :L], bs_t, mk, mv, mqn.reshape(1, HEAD_DIM),
                                 L, t_valid)
    x_new = _out_proj(x2, [y.reshape(b * t, -1)], [w_out_bf], tm).reshape(b, t, d)
    return x_new, (c, n, mb[:, :, 0]), vbn


def _odd_common(x3, mk, mv, prm, tabs, v_transposed):
    norm_g, w_in_bf, qn, kn, w_out_c, w_out_x, mqn = prm
    b, t, d = x3.shape
    x2 = x3.reshape(b * t, d)
    tm = _row_tile(b * t)
    h3 = _in_proj(x2, norm_g, w_in_bf, tm).reshape(b, t, O_COLS)
    tq = math.gcd(t, 256)
    pre = _odd_pre(h3, tabs, qn.reshape(1, HEAD_DIM), kn.reshape(1, HEAD_DIM), mk, mv, mqn.reshape(1, HEAD_DIM), tq,
                   v_transposed)
    return x2, tm, h3, pre


def _finish_odd(x2, tm, yc2, yx, prm, shape):
    w_out_c, w_out_x = prm[4], prm[5]
    return _out_proj(x2, [yc2, yx.reshape(x2.shape[0], XA_W)], [w_out_c, w_out_x], tm).reshape(shape)


def kernel(x_prompt, x_sample, state_mlstm_C, state_mlstm_n, state_mlstm_m, cache_attn_k, cache_attn_v,
           cache_idx_k, cache_mem_k, cache_mem_v, page_table, mem_prompt, norm_even, w_in_even, b_if_even,
           mlstm_hnorm, gmlp_ln_g, gmlp_ln_b, gmlp_w, gmlp_b, w_out_even, norm_odd, w_in_odd, attn_qn, attn_kn,
           w_out_odd, w_mem_kv, mem_qn, mem_kn):
    B, S, D = x_prompt.shape
    Bd, T, _ = x_sample.shape
    depth = w_mem_kv.shape[0]
    n_mem = mem_prompt.shape[1]
    past = page_table.shape[1] * PAGE_SIZE
    assert S % LANES == 0 and T <= SAMPLE_PAD

    we = w_in_even
    we = jnp.concatenate([we[..., :2048], we[..., 2056:], we[..., 2048:2056],
                          jnp.zeros(we.shape[:2] + (LANES - 2 * ML_HEADS,), we.dtype)], axis=-1).astype(BF16)
    wo = w_in_odd
    wo = jnp.concatenate([wo[..., 0:1024], wo[..., 1536:2560], wo[..., 1024:1536], wo[..., 2560:3072],
                          wo[..., 3144:4168], wo[..., 3080:3144], wo[..., 3072:3080],
                          jnp.zeros(wo.shape[:2] + (LANES - IDX_DIM - IDX_HEADS,), wo.dtype)], axis=-1).astype(BF16)
    w_out_even_bf = w_out_even.astype(BF16)
    w_out_odd_bf = w_out_odd.astype(BF16)
    bif_pad = jnp.pad(b_if_even, ((0, 0), (0, LANES - 2 * ML_HEADS)))[:, None, :]

    mem_k_p, mem_v_p = _mem_kv(mem_prompt.reshape(B * n_mem, D), w_mem_kv.astype(BF16), mem_kn,
                               _row_tile(B * n_mem))
    mem_k_p = mem_k_p.reshape(depth, B, n_mem, XA_W)
    mem_v_p = mem_v_p.reshape(depth, B, n_mem, XA_W)
    cmk = cache_mem_k.reshape(depth, Bd, n_mem, XA_W)
    cmv = cache_mem_v.reshape(depth, Bd, n_mem, XA_W)

    tp = SAMPLE_PAD
    xp = x_prompt
    xs = jnp.pad(x_sample, ((0, 0), (0, tp - T), (0, 0)))
    tabs_p = _rope_tables(jnp.arange(S))
    tabs_s = _rope_tables(past + jnp.arange(tp))
    zero_c = jnp.zeros((B, ML_HEADS, HEAD_DIM, HEAD_DIM), F32)
    zero_n = jnp.zeros((B, ML_HEADS, HEAD_DIM), F32)
    zero_m = jnp.zeros((B, ML_HEADS), F32)
    topk_p = min(TOPK_MAX, S // 4)
    topk_s = min(TOPK_MAX, (past + T) // 4)

    mlC_p, mln_p, mlm_p, mlC_s, mln_s, mlm_s, gv_s = [], [], [], [], [], [], []
    ak_p, av_p, ik_p, ak_s, av_s, ik_s = [], [], [], [], [], []
    for l in range(depth):
        if l % 2 == 0:
            e = l // 2
            prm = (norm_even[e], we[e], bif_pad[e], mlstm_hnorm[e], gmlp_ln_g[e], gmlp_ln_b[e], gmlp_w[e], gmlp_b[e],
                   w_out_even_bf[e], mem_qn[l])
            xp, (c, n, m), _ = _even_layer(xp, mem_k_p[l], mem_v_p[l], zero_c, zero_n, zero_m, prm, LANES)
            mlC_p.append(c); mln_p.append(n); mlm_p.append(m)
            xs, (c, n, m), vrows = _even_layer(xs, cmk[l], cmv[l], state_mlstm_C[e], state_mlstm_n[e],
                                               state_mlstm_m[e], prm, T)
            mlC_s.append(c); mln_s.append(n); mlm_s.append(m)
            gv_s.append(vrows[:, :T])
        else:
            o = l // 2
            prm = (norm_odd[o], wo[o], attn_qn[o], attn_kn[o], w_out_odd_bf[o, :AT_W], w_out_odd_bf[o, AT_W:],
                   mem_qn[l])
            x2, tm, h3, pre = _odd_common(xp, mem_k_p[l], mem_v_p[l], prm, tabs_p, True)
            q_bf, k, v, k_bf, vt_bf, qi_bf, ki, ki_bf, wi, yx = pre
            yc = _dsa_prompt(q_bf, qi_bf, wi, h3, k_bf, vt_bf, ki_bf, topk_p)
            xp = _finish_odd(x2, tm, yc.reshape(B * S, AT_W), yx, prm, xp.shape)
            ak_p.append(k.reshape(B, S, AT_KV_HEADS, HEAD_DIM))
            av_p.append(v.reshape(B, S, AT_KV_HEADS, HEAD_DIM))
            ik_p.append(ki)
            x2, tm, h3, pre = _odd_common(xs, cmk[l], cmv[l], prm, tabs_s, False)
            q_bf, k, v, k_bf, v_bf, qi_bf, ki, ki_bf, wi, yx = pre
            q_rows = q_bf[:, :T].reshape(Bd, T * AT_HEADS, HEAD_DIM)
            gc_rows = h3[:, :T, O_GC:O_GC + AT_W].reshape(Bd, T * AT_HEADS, HEAD_DIM)
            qi_st = qi_bf.reshape(Bd, tp, IDX_HEADS, IDX_DIM).transpose(0, 2, 1, 3).reshape(
                Bd, IDX_HEADS * tp, IDX_DIM)
            wi_h = wi[:, :, IDX_DIM:IDX_DIM + IDX_HEADS] * IDX_DIM ** -0.5
            wi_st = jnp.broadcast_to(wi_h.transpose(0, 2, 1).reshape(Bd, IDX_HEADS * tp, 1),
                                     (Bd, IDX_HEADS * tp, LANES))
            kv_page = lambda a: jnp.pad(a.reshape(Bd, tp * AT_KV_HEADS, HEAD_DIM),
                                        ((0, 0), (0, AT_KV_HEADS * (PAGE_SIZE - tp)), (0, 0)))
            ki_page = jnp.pad(ki_bf, ((0, 0), (0, PAGE_SIZE - tp), (0, 0)))
            y_rows = _dsa_sample(q_rows, qi_st, wi_st, gc_rows, kv_page(k), kv_page(v), ki_page,
                                 cache_attn_k, cache_attn_v, cache_idx_k, o, page_table, topk_s, tp, T)
            yc = jnp.pad(y_rows.reshape(Bd, T, AT_W), ((0, 0), (0, tp - T), (0, 0))).reshape(Bd * tp, AT_W)
            xs = _finish_odd(x2, tm, yc, yx, prm, xs.shape)
            ak_s.append(k[:, :T].reshape(Bd, T, AT_KV_HEADS, HEAD_DIM))
            av_s.append(v[:, :T].reshape(Bd, T, AT_KV_HEADS, HEAD_DIM))
            ik_s.append(ki[:, :T])
    return (xp, xs[:, :T],
            jnp.stack(mlC_p), jnp.stack(mln_p), jnp.stack(mlm_p),
            jnp.stack(mlC_s), jnp.stack(mln_s), jnp.stack(mlm_s),
            jnp.stack(gv_s),
            jnp.stack(ak_p), jnp.stack(av_p), jnp.stack(ik_p),
            jnp.stack(ak_s), jnp.stack(av_s), jnp.stack(ik_s),
            mem_k_p.reshape(depth, B, n_mem, XA_HEADS, HEAD_DIM),
            mem_v_p.reshape(depth, B, n_mem, XA_HEADS, HEAD_DIM))
```

```python
import functools
import math

import jax
import jax.numpy as jnp
from jax import lax
from jax.experimental import pallas as pl
from jax.experimental.pallas import tpu as pltpu

F32 = jnp.float32
BF16 = jnp.bfloat16
I32 = jnp.int32

HEAD_DIM = 128
ML_HEADS = 4
GM_GROUPS = 4
AT_HEADS = 8
AT_KV_HEADS = 2
AT_GROUP = AT_HEADS // AT_KV_HEADS
IDX_HEADS = 8
IDX_DIM = 64
XA_HEADS = 4
TOPK_MAX = 256
PAGE_SIZE = 128
ROPE_THETA = 500000.0
ROPE_FRAC = 4
NORM_EPS = 1e-6

ML_W = ML_HEADS * HEAD_DIM
GM_W = GM_GROUPS * HEAD_DIM
XA_W = XA_HEADS * HEAD_DIM
AT_W = AT_HEADS * HEAD_DIM
KV_W = AT_KV_HEADS * HEAD_DIM
IDX_W = IDX_HEADS * IDX_DIM

LANES = 128
SUBLANES = 8
VMEM_LIMIT = 56 * 1024 * 1024
NEG = -1e30
NO_SCORE = -3.0e38
BISECT_STEPS = 16
SAMPLE_PAD = SUBLANES
BF16_SUBLANES = 16
VT_ROWS = HEAD_DIM + BF16_SUBLANES
Q_SCALE = HEAD_DIM ** -0.5 * math.log2(math.e)
PAGES_PER_STEP = 16
COUNT_ACC_TILES = 4
GROUP_SHIFT = AT_GROUP.bit_length() - 1
assert AT_GROUP == 1 << GROUP_SHIFT and AT_KV_HEADS & (AT_KV_HEADS - 1) == 0 and XA_HEADS & (XA_HEADS - 1) == 0

E_Q, E_K, E_V, E_O, E_GA, E_U, E_VB, E_GB, E_XQ, E_GX, E_IF = (
    0, 512, 1024, 1536, 2048, 2560, 3072, 3584, 4096, 4608, 5120)
E_MAIN = 5120
E_COLS = E_MAIN + LANES
O_Q, O_GC, O_K, O_V, O_QI, O_XQ, O_GX, O_KI = 0, 1024, 2048, 2304, 2560, 3072, 3584, 4096
O_MAIN = 4096
O_COLS = O_MAIN + LANES


def _cparams(sem):
    return pltpu.CompilerParams(dimension_semantics=sem, vmem_limit_bytes=VMEM_LIMIT)


def _sigmoid(x):
    return 1.0 / (1.0 + jnp.exp(-x))


def _silu(x):
    return x * _sigmoid(x)


def _rms_rows(x, g):
    return x * lax.rsqrt(jnp.mean(x * x, axis=-1, keepdims=True) + NORM_EPS) * g


def _dot_nt(a, b):
    return lax.dot_general(a, b, (((1,), (1,)), ((), ())), preferred_element_type=F32)


def _dot_tn(a, b):
    return lax.dot_general(a, b, (((0,), (0,)), ((), ())), preferred_element_type=F32)


def _dot(a, b):
    return jnp.dot(a, b, preferred_element_type=F32)


def _in_proj_kernel(x_ref, g_ref, w_ref, o_ref, *, col_chunk):
    xn = _rms_rows(x_ref[...], g_ref[...]).astype(BF16)
    n = o_ref.shape[-1]
    for c0 in range(0, n, col_chunk):
        c1 = min(c0 + col_chunk, n)
        o_ref[:, c0:c1] = _dot(xn, w_ref[:, c0:c1])


def _in_proj(x2, g, w_bf, tm):
    m, d = x2.shape
    n = w_bf.shape[1]
    return pl.pallas_call(
        functools.partial(_in_proj_kernel, col_chunk=512),
        grid=(m // tm,),
        in_specs=[pl.BlockSpec((tm, d), lambda i: (i, 0)),
                  pl.BlockSpec((1, d), lambda i: (0, 0)),
                  pl.BlockSpec((d, n), lambda i: (0, 0))],
        out_specs=pl.BlockSpec((tm, n), lambda i: (i, 0)),
        out_shape=jax.ShapeDtypeStruct((m, n), F32),
        compiler_params=_cparams(("parallel",)),
        name="in_proj",
    )(x2, g.reshape(1, d), w_bf)


def _out_proj_kernel(*refs, n_in):
    x_ref = refs[0]
    y_refs = refs[1:1 + n_in]
    w_refs = refs[1 + n_in:1 + 2 * n_in]
    o_ref = refs[1 + 2 * n_in]
    acc = x_ref[...]
    for y_ref, w_ref in zip(y_refs, w_refs):
        acc = acc + _dot(y_ref[...].astype(BF16), w_ref[...])
    o_ref[...] = acc


def _out_proj(x2, ys, ws, tm):
    m, d = x2.shape
    n_in = len(ys)
    in_specs = [pl.BlockSpec((tm, d), lambda i: (i, 0))]
    in_specs += [pl.BlockSpec((tm, y.shape[1]), lambda i: (i, 0)) for y in ys]
    in_specs += [pl.BlockSpec(w.shape, lambda i: (0, 0)) for w in ws]
    return pl.pallas_call(
        functools.partial(_out_proj_kernel, n_in=n_in),
        grid=(m // tm,),
        in_specs=in_specs,
        out_specs=pl.BlockSpec((tm, d), lambda i: (i, 0)),
        out_shape=jax.ShapeDtypeStruct((m, d), F32),
        compiler_params=_cparams(("parallel",)),
        name="out_proj",
    )(x2, *ys, *ws)


def _mem_kv_kernel(mem_ref, w_ref, kn_ref, k_ref, v_ref):
    kv = _dot(mem_ref[...].astype(BF16), w_ref[...])
    kn = kn_ref[...]
    for h in range(XA_HEADS):
        sl = slice(h * HEAD_DIM, (h + 1) * HEAD_DIM)
        k_ref[:, sl] = _rms_rows(kv[:, sl], kn)
    v_ref[...] = kv[:, XA_W:]


def _mem_kv(mem2, w_kv_bf, mem_kn, tm):
    m, d = mem2.shape
    depth = w_kv_bf.shape[0]
    out = jax.ShapeDtypeStruct((depth, m, XA_W), F32)
    return pl.pallas_call(
        _mem_kv_kernel,
        grid=(depth, m // tm),
        in_specs=[pl.BlockSpec((tm, d), lambda l, i: (i, 0)),
                  pl.BlockSpec((None, d, 2 * XA_W), lambda l, i: (l, 0, 0)),
                  pl.BlockSpec((None, 1, HEAD_DIM), lambda l, i: (l, 0, 0))],
        out_specs=[pl.BlockSpec((None, tm, XA_W), lambda l, i: (l, i, 0)),
                   pl.BlockSpec((None, tm, XA_W), lambda l, i: (l, i, 0))],
        out_shape=[out, out],
        compiler_params=_cparams(("parallel", "parallel")),
        name="mem_kv",
    )(mem2, w_kv_bf, mem_kn.reshape(depth, 1, HEAD_DIM))


def _mem_attn_rows(xq, gx, mk_ref, mv_ref, qn):
    n_q = xq.shape[0]
    q = jnp.concatenate([_rms_rows(xq[:, h * HEAD_DIM:(h + 1) * HEAD_DIM], qn) for h in range(XA_HEADS)],
                        axis=0).astype(BF16)
    s = _dot_nt(q, mk_ref[...].astype(BF16)) * HEAD_DIM ** -0.5
    col_head = lax.broadcasted_iota(I32, (n_q, s.shape[1]), 1) & (XA_HEADS - 1)
    s = s + jnp.concatenate([jnp.where(col_head == h, 0.0, NEG) for h in range(XA_HEADS)], axis=0)
    s = s - jnp.max(s, axis=-1, keepdims=True)
    p = jnp.exp(s)
    p = p / jnp.sum(p, axis=-1, keepdims=True)
    o = _dot(p.astype(BF16), mv_ref[...].astype(BF16))
    return [o[h * n_q:(h + 1) * n_q] * _silu(gx[:, h * HEAD_DIM:(h + 1) * HEAD_DIM]) for h in range(XA_HEADS)]


def _mem_attn_heads(xq, gx, mk_ref, mv_ref, qn):
    if mk_ref.shape[-1] == HEAD_DIM:
        return _mem_attn_rows(xq, gx, mk_ref, mv_ref, qn)
    outs = []
    for h in range(XA_HEADS):
        sl = slice(h * HEAD_DIM, (h + 1) * HEAD_DIM)
        q = _rms_rows(xq[:, sl], qn).astype(BF16)
        s = _dot_nt(q, mk_ref[:, sl].astype(BF16)) * HEAD_DIM ** -0.5
        s = s - jnp.max(s, axis=-1, keepdims=True)
        p = jnp.exp(s)
        p = p / jnp.sum(p, axis=-1, keepdims=True)
        o = _dot(p.astype(BF16), mv_ref[:, sl].astype(BF16))
        outs.append(o * _silu(gx[:, sl]))
    return outs


def _even_mix_kernel(h_ref, if_ref, c0_ref, n0_ref, m0_ref, bif_ref, hg_ref, lng_ref, lnb_ref,
                     ws_ref, bs_ref, mk_ref, mv_ref, qn_ref,
                     y_ref, vbn_ref, c_ref, n_ref, m_ref, *, t_valid):
    L = h_ref.shape[0]
    ci = pl.program_id(1)

    @pl.when(ci == 0)
    def _():
        c_ref[...] = c0_ref[...]
        n_ref[...] = n0_ref[...]
        m_ref[...] = m0_ref[...]

    row = lax.broadcasted_iota(I32, (L, L), 0)
    col = lax.broadcasted_iota(I32, (L, L), 1)
    eye = row == col
    row_le_col = row <= col
    col_le_row = col <= row
    tcol = lax.broadcasted_iota(I32, (L, 1), 0)

    gates = if_ref[...] + bif_ref[...]
    log_f_all = jnp.minimum(gates, 0.0) - jnp.log(1.0 + jnp.exp(-jnp.abs(gates)))

    for h in range(ML_HEADS):
        sl = slice(h * HEAD_DIM, (h + 1) * HEAD_DIM)
        li_col = gates[:, h:h + 1]
        lf_col = log_f_all[:, ML_HEADS + h:ML_HEADS + h + 1]
        if t_valid < L:
            li_col = jnp.where(tcol < t_valid, li_col, NEG)
            lf_col = jnp.where(tcol < t_valid, lf_col, 0.0)
        li_b = jnp.broadcast_to(li_col, (L, L))
        lf_b = jnp.broadcast_to(lf_col, (L, L))
        li_row = jnp.sum(jnp.where(eye, li_b, 0.0), axis=0, keepdims=True)
        lf_row = jnp.sum(jnp.where(eye, lf_b, 0.0), axis=0, keepdims=True)
        b_row = jnp.sum(jnp.where(row_le_col, lf_b, 0.0), axis=0, keepdims=True)
        b_col = jnp.sum(jnp.where(col_le_row, jnp.broadcast_to(lf_row, (L, L)), 0.0),
                        axis=1, keepdims=True)
        m_prev = m_ref[h:h + 1, 0:1]
        n_prev = n_ref[h:h + 1, :]
        c_prev = c_ref[h]

        dlog = jnp.where(col_le_row, b_col - b_row + li_row, NEG)
        inter = b_col + m_prev
        m_t = jnp.maximum(inter, jnp.max(dlog, axis=1, keepdims=True))
        w_intra = jnp.exp(dlog - m_t)
        w_inter = jnp.exp(inter - m_t)
        q = h_ref[:, E_Q + h * HEAD_DIM:E_Q + (h + 1) * HEAD_DIM]
        k = h_ref[:, E_K + h * HEAD_DIM:E_K + (h + 1) * HEAD_DIM] * HEAD_DIM ** -0.5
        v = h_ref[:, E_V + h * HEAD_DIM:E_V + (h + 1) * HEAD_DIM]
        q_bf = q.astype(BF16)
        k_bf = k.astype(BF16)
        qk = _dot_nt(q_bf, k_bf) * w_intra
        num = _dot(qk.astype(BF16), v.astype(BF16)) + w_inter * _dot(q_bf, c_prev.astype(BF16))
        den = jnp.sum(qk, axis=1, keepdims=True) + w_inter * jnp.sum(q * n_prev, axis=1, keepdims=True)
        hh = num / jnp.maximum(jnp.abs(den), jnp.exp(-m_t))

        b_end = b_col[L - 1:L, :]
        g = b_end - b_col + li_col
        m_new = jnp.maximum(b_end + m_prev, jnp.max(g, axis=0, keepdims=True))
        a_prev = jnp.exp(b_end + m_prev - m_new)
        a_tok = jnp.exp(g - m_new)
        c_ref[h] = a_prev * c_prev + _dot_tn(k_bf, (a_tok * v).astype(BF16))
        n_ref[h:h + 1, :] = a_prev * n_prev + jnp.sum(a_tok * k, axis=0, keepdims=True)
        m_ref[h:h + 1, :] = jnp.broadcast_to(m_new, (1, LANES))

        ha = _rms_rows(hh, hg_ref[:, sl])
        y_ref[:, sl] = ha * _sigmoid(h_ref[:, E_O + h * HEAD_DIM:E_O + (h + 1) * HEAD_DIM]) * _silu(
            h_ref[:, E_GA + h * HEAD_DIM:E_GA + (h + 1) * HEAD_DIM])

    vb = h_ref[:, E_VB:E_VB + GM_W]
    vc = vb - jnp.mean(vb, axis=-1, keepdims=True)
    var = jnp.mean(vc * vc, axis=-1, keepdims=True)
    vbn = vc * lax.rsqrt(var + NORM_EPS) * lng_ref[...] + lnb_ref[...]
    vbn_ref[...] = vbn
    for g_i in range(GM_GROUPS):
        sl = slice(g_i * HEAD_DIM, (g_i + 1) * HEAD_DIM)
        w = jnp.where(col_le_row, ws_ref[g_i], 0.0).astype(BF16)
        mixed = _dot(w, vbn[:, sl].astype(BF16)) + bs_ref[:, g_i:g_i + 1]
        y_ref[:, ML_W + g_i * HEAD_DIM:ML_W + (g_i + 1) * HEAD_DIM] = (
            h_ref[:, E_U + g_i * HEAD_DIM:E_U + (g_i + 1) * HEAD_DIM] * mixed
            * _silu(h_ref[:, E_GB + g_i * HEAD_DIM:E_GB + (g_i + 1) * HEAD_DIM]))

    outs = _mem_attn_heads(h_ref[:, E_XQ:E_XQ + XA_W], h_ref[:, E_GX:E_GX + XA_W], mk_ref, mv_ref, qn_ref[...])
    for h, o in enumerate(outs):
        y_ref[:, ML_W + GM_W + h * HEAD_DIM:ML_W + GM_W + (h + 1) * HEAD_DIM] = o


def _even_mix(h3, c0, n0, m0b, bif, hg, lng, lnb, ws, bs_t, mk, mv, qn, L, t_valid):
    b, t, _ = h3.shape
    nc = t // L
    full2 = lambda bi, ci: (0, 0)
    return pl.pallas_call(
        functools.partial(_even_mix_kernel, t_valid=t_valid),
        grid=(b, nc),
        in_specs=[pl.BlockSpec((None, L, E_MAIN), lambda bi, ci: (bi, ci, 0)),
                  pl.BlockSpec((None, L, LANES), lambda bi, ci: (bi, ci, E_MAIN // LANES)),
                  pl.BlockSpec((None, ML_HEADS, HEAD_DIM, HEAD_DIM), lambda bi, ci: (bi, 0, 0, 0)),
                  pl.BlockSpec((None, ML_HEADS, HEAD_DIM), lambda bi, ci: (bi, 0, 0)),
                  pl.BlockSpec((None, ML_HEADS, LANES), lambda bi, ci: (bi, 0, 0)),
                  pl.BlockSpec((1, LANES), full2),
                  pl.BlockSpec((1, ML_W), full2),
                  pl.BlockSpec((1, GM_W), full2),
                  pl.BlockSpec((1, GM_W), full2),
                  pl.BlockSpec((GM_GROUPS, L, L), lambda bi, ci: (0, 0, 0)),
                  pl.BlockSpec((L, LANES), full2),
                  pl.BlockSpec((None,) + mk.shape[1:], lambda bi, ci: (bi, 0, 0)),
                  pl.BlockSpec((None,) + mv.shape[1:], lambda bi, ci: (bi, 0, 0)),
                  pl.BlockSpec((1, HEAD_DIM), full2)],
        out_specs=[pl.BlockSpec((None, L, ML_W + GM_W + XA_W), lambda bi, ci: (bi, ci, 0)),
                   pl.BlockSpec((None, L, GM_W), lambda bi, ci: (bi, ci, 0)),
                   pl.BlockSpec((None, ML_HEADS, HEAD_DIM, HEAD_DIM), lambda bi, ci: (bi, 0, 0, 0)),
                   pl.BlockSpec((None, ML_HEADS, HEAD_DIM), lambda bi, ci: (bi, 0, 0)),
                   pl.BlockSpec((None, ML_HEADS, LANES), lambda bi, ci: (bi, 0, 0))],
        out_shape=[jax.ShapeDtypeStruct((b, t, ML_W + GM_W + XA_W), F32),
                   jax.ShapeDtypeStruct((b, t, GM_W), F32),
                   jax.ShapeDtypeStruct((b, ML_HEADS, HEAD_DIM, HEAD_DIM), F32),
                   jax.ShapeDtypeStruct((b, ML_HEADS, HEAD_DIM), F32),
                   jax.ShapeDtypeStruct((b, ML_HEADS, LANES), F32)],
        compiler_params=_cparams(("parallel", "arbitrary")),
        name="even_mix",
    )(h3, h3, c0, n0, m0b, bif, hg, lng, lnb, ws, bs_t, mk, mv, qn)


def _rope_lanes(x, cos, sin_lo, sin_hi, half):
    n = x.shape[-1]
    return x * cos + pltpu.roll(x, n - half, 1) * sin_lo + pltpu.roll(x, half, 1) * sin_hi


def _odd_pre_kernel(h_ref, kiwi_ref, ca_ref, sa1_ref, sa2_ref, ci_ref, si1_ref, si2_ref,
                    qn_ref, kn_ref, mk_ref, mv_ref, mqn_ref,
                    q_ref, k_ref, v_ref, kbf_ref, vbf_ref, qi_ref, ki_ref, kibf_ref, wi_ref, yx_ref, *, v_transposed):
    ca, sa1, sa2 = ca_ref[...], sa1_ref[...], sa2_ref[...]
    ci, si1, si2 = ci_ref[...], si1_ref[...], si2_ref[...]
    a_half = HEAD_DIM // ROPE_FRAC // 2
    i_half = IDX_DIM // ROPE_FRAC // 2
    for h in range(AT_HEADS):
        sl = slice(O_Q + h * HEAD_DIM, O_Q + (h + 1) * HEAD_DIM)
        qh = _rope_lanes(_rms_rows(h_ref[:, sl], qn_ref[...]), ca, sa1, sa2, a_half)
        q_ref[:, h * HEAD_DIM:(h + 1) * HEAD_DIM] = (qh * Q_SCALE).astype(BF16)
    for h in range(AT_KV_HEADS):
        sl = slice(O_K + h * HEAD_DIM, O_K + (h + 1) * HEAD_DIM)
        kh = _rope_lanes(_rms_rows(h_ref[:, sl], kn_ref[...]), ca, sa1, sa2, a_half)
        k_ref[:, h * HEAD_DIM:(h + 1) * HEAD_DIM] = kh
        kbf_ref[:, h * HEAD_DIM:(h + 1) * HEAD_DIM] = kh.astype(BF16)
    v = h_ref[:, O_V:O_V + KV_W]
    v_ref[...] = v
    if v_transposed:
        tail_row = lax.broadcasted_iota(I32, (VT_ROWS - HEAD_DIM, v.shape[0]), 0)
        tail = jnp.where(tail_row == 0, 1.0, 0.0).astype(BF16)
        for n in range(AT_KV_HEADS):
            vbf_ref[n, :HEAD_DIM, :] = v[:, n * HEAD_DIM:(n + 1) * HEAD_DIM].T.astype(BF16)
            vbf_ref[n, HEAD_DIM:, :] = tail
    else:
        vbf_ref[...] = v.astype(BF16)
    for j in range(IDX_W // LANES):
        sl = slice(O_QI + j * LANES, O_QI + (j + 1) * LANES)
        qi_ref[:, j * LANES:(j + 1) * LANES] = _rope_lanes(h_ref[:, sl], ci, si1, si2, i_half).astype(BF16)
    kiwi = kiwi_ref[...]
    ki = _rope_lanes(kiwi, ci, si1, si2, i_half)[:, :IDX_DIM]
    ki_ref[...] = ki
    kibf_ref[...] = ki.astype(BF16)
    lane = lax.broadcasted_iota(I32, kiwi.shape, 1)
    wsel = (lane >= IDX_DIM) & (lane < IDX_DIM + IDX_HEADS)
    wi_ref[...] = jnp.where(wsel, kiwi * IDX_HEADS ** -0.5, 0.0)
    outs = _mem_attn_heads(h_ref[:, O_XQ:O_XQ + XA_W], h_ref[:, O_GX:O_GX + XA_W], mk_ref, mv_ref, mqn_ref[...])
    for h, o in enumerate(outs):
        yx_ref[:, h * HEAD_DIM:(h + 1) * HEAD_DIM] = o


def _odd_pre(h3, tabs, qn, kn, mk, mv, mqn, tq, v_transposed):
    b, t, _ = h3.shape
    row_blk = lambda w: pl.BlockSpec((None, tq, w), lambda bi, ti: (bi, ti, 0))
    tab = pl.BlockSpec((tq, LANES), lambda bi, ti: (ti, 0))
    vec = pl.BlockSpec((1, HEAD_DIM), lambda bi, ti: (0, 0))
    mem = pl.BlockSpec((None,) + mk.shape[1:], lambda bi, ti: (bi, 0, 0))
    sds = lambda w, dt: jax.ShapeDtypeStruct((b, t, w), dt)
    if v_transposed:
        vbf_spec = pl.BlockSpec((None, AT_KV_HEADS, VT_ROWS, tq), lambda bi, ti: (bi, 0, 0, ti))
        vbf_shape = jax.ShapeDtypeStruct((b, AT_KV_HEADS, VT_ROWS, t), BF16)
    else:
        vbf_spec, vbf_shape = row_blk(KV_W), sds(KV_W, BF16)
    return pl.pallas_call(
        functools.partial(_odd_pre_kernel, v_transposed=v_transposed),
        grid=(b, t // tq),
        in_specs=[row_blk(O_MAIN),
                  pl.BlockSpec((None, tq, LANES), lambda bi, ti: (bi, ti, O_MAIN // LANES)),
                  tab, tab, tab, tab, tab, tab, vec, vec, mem, mem, vec],
        out_specs=[row_blk(AT_W), row_blk(KV_W), row_blk(KV_W), row_blk(KV_W), vbf_spec,
                   row_blk(IDX_W), row_blk(IDX_DIM), row_blk(IDX_DIM), row_blk(LANES), row_blk(XA_W)],
        out_shape=[sds(AT_W, BF16), sds(KV_W, F32), sds(KV_W, F32), sds(KV_W, BF16), vbf_shape,
                   sds(IDX_W, BF16), sds(IDX_DIM, F32), sds(IDX_DIM, BF16), sds(LANES, F32), sds(XA_W, F32)],
        compiler_params=_cparams(("parallel", "parallel")),
        name="odd_pre",
    )(h3, h3, *tabs, qn, kn, mk, mv, mqn)


def _rope_tables(pos):
    pos = pos.astype(F32)[:, None]

    def build(width, reps):
        rd = width // ROPE_FRAC
        half = rd // 2
        inv = ROPE_THETA ** (-jnp.arange(half, dtype=F32) / half)
        ang = pos * inv[None, :]
        cos, sin = jnp.cos(ang), jnp.sin(ang)
        ones = jnp.ones((pos.shape[0], width - rd), F32)
        zeros_h = jnp.zeros((pos.shape[0], half), F32)
        zeros_r = jnp.zeros((pos.shape[0], width - rd), F32)
        c = jnp.concatenate([cos, cos, ones], axis=1)
        s_lo = jnp.concatenate([-sin, zeros_h, zeros_r], axis=1)
        s_hi = jnp.concatenate([zeros_h, sin, zeros_r], axis=1)
        return [jnp.tile(a, (1, reps)) for a in (c, s_lo, s_hi)]

    return build(HEAD_DIM, 1) + build(IDX_DIM, LANES // IDX_DIM)


def _topk_threshold(score_ref, idx, n_chunks, chunk_shape, key_axis, k, pos_bits, stats=None):
    red_shape = tuple(1 if a == key_axis else d for a, d in enumerate(chunk_shape))
    pos_in_chunk = lax.broadcasted_iota(I32, chunk_shape, key_axis)
    acc_rows = min(chunk_shape[0], COUNT_ACC_TILES * SUBLANES) if key_axis == 0 else chunk_shape[0]
    acc_shape = (acc_rows, chunk_shape[1])
    zeros = jnp.zeros(acc_shape, F32)

    def folded(x, op=jnp.add):
        parts = [x[r:r + acc_rows] for r in range(0, chunk_shape[0], acc_rows)]
        while len(parts) > 1:
            parts = [op(a, b) for a, b in zip(parts[::2], parts[1::2])] + ([parts[-1]] if len(parts) % 2 else [])
        return parts[0]

    def count(pred):
        def body(c, acc):
            return acc + folded(jnp.where(pred(score_ref[idx(c)], c), 1.0, 0.0))

        return jnp.sum(lax.fori_loop(0, n_chunks, body, zeros), axis=key_axis, keepdims=True)

    def count_ge(t):
        t_b = jnp.broadcast_to(t, chunk_shape)
        return count(lambda s, c: s >= t_b)

    def range_body(c, carry):
        top, bottom, valid = carry
        s = score_ref[idx(c)]
        ok = s > NO_SCORE
        return (jnp.maximum(top, folded(s, jnp.maximum)),
                jnp.minimum(bottom, folded(jnp.where(ok, s, -NO_SCORE), jnp.minimum)),
                valid + folded(jnp.where(ok, 1.0, 0.0)))

    if stats is None:
        stats = lax.fori_loop(0, n_chunks, range_body,
                              (jnp.full(acc_shape, NO_SCORE, F32), jnp.full(acc_shape, -NO_SCORE, F32), zeros))
    top, bottom, valid = stats
    top = jnp.max(top, axis=key_axis, keepdims=True)
    bottom = jnp.min(bottom, axis=key_axis, keepdims=True)
    rank = jnp.minimum(jnp.sum(valid, axis=key_axis, keepdims=True), k)

    def bisect_step(i, carry):
        lo, mid_hi, hi = carry
        mid = 0.5 * (lo + mid_hi)
        enough = count_ge(mid) >= rank
        return jnp.where(enough, mid, lo), jnp.where(enough, mid_hi, mid), jnp.where(enough, hi, mid)

    _, _, hi = lax.fori_loop(0, BISECT_STEPS, bisect_step, (bottom, top, jnp.full(red_shape, -NO_SCORE, F32)))

    def walk_cond(carry):
        return jnp.max(carry[2]) > 0.0

    def walk_body(carry):
        hi, thr, todo = carry
        hi_b = jnp.broadcast_to(hi, chunk_shape)

        def body(c, acc):
            s = score_ref[idx(c)]
            return jnp.maximum(acc, folded(jnp.where(s < hi_b, s, NO_SCORE), jnp.maximum))

        cand = jnp.max(lax.fori_loop(0, n_chunks, body, jnp.full(acc_shape, NO_SCORE, F32)),
                       axis=key_axis, keepdims=True)
        found = count_ge(cand) >= rank
        open_ = todo > 0.0
        return (jnp.where(open_, cand, hi), jnp.where(open_, cand, thr),
                jnp.where(open_ & jnp.logical_not(found), 1.0, 0.0))

    _, thr, _ = lax.while_loop(walk_cond, walk_body, (hi, bottom, jnp.ones(red_shape, F32)))
    thr_b = jnp.broadcast_to(thr, chunk_shape)

    def count_body(c, carry):
        gt, eq = carry
        s = score_ref[idx(c)]
        return (gt + folded(jnp.where(s > thr_b, 1.0, 0.0)), eq + folded(jnp.where(s == thr_b, 1.0, 0.0)))

    gt, eq = lax.fori_loop(0, n_chunks, count_body, (zeros, zeros))
    need = rank - jnp.sum(gt, axis=key_axis, keepdims=True)
    excess = jnp.sum(eq, axis=key_axis, keepdims=True) - need

    def pos_step(i, cut):
        cand = cut + lax.shift_left(jnp.int32(1), pos_bits - 1 - i)
        cand_b = jnp.broadcast_to(cand, chunk_shape)
        cnt = count(lambda s, c: (s == thr_b) & (c * chunk_shape[key_axis] + pos_in_chunk < cand_b))
        return jnp.where(cnt < need, cand, cut)

    any_excess = jnp.max(excess) > 0.0
    cut = lax.cond(any_excess,
                   lambda: lax.fori_loop(0, pos_bits, pos_step, jnp.zeros(red_shape, I32)),
                   lambda: jnp.full(red_shape, 2 ** pos_bits, I32))
    return thr, cut, any_excess


def _selected(scores, thr, cut, pos):
    return (scores > thr) | ((scores == thr) & (pos <= cut))


def _dsa_prompt_kernel(q_ref, qi_ref, wi_ref, gc_ref, k_ref, vt_ref, ki_ref, o_ref,
                       sc_ref, qt_ref, qit_ref, wit_ref, m_ref, l_ref, acc_ref, *, topk, pos_bits, ck, ca):
    qb = pl.program_id(1)
    nq = q_ref.shape[0]
    n_chunks = (qb * nq) // ck + 1
    chunk = (ck, LANES)
    stat_rows = COUNT_ACC_TILES * SUBLANES
    kpos_in = lax.broadcasted_iota(I32, chunk, 0)
    qpos = qb * nq + lax.broadcasted_iota(I32, chunk, 1)

    for h in range(AT_HEADS):
        sl = slice(h * HEAD_DIM, (h + 1) * HEAD_DIM)
        qt_ref[:, sl] = q_ref[:, sl].astype(F32).T.astype(BF16)
    for j in range(IDX_W // LANES):
        t = qi_ref[:, j * LANES:(j + 1) * LANES].astype(F32).T.astype(BF16)
        for i in range(LANES // IDX_DIM):
            h = j * (LANES // IDX_DIM) + i
            qit_ref[:, h * LANES:(h + 1) * LANES] = t[i * IDX_DIM:(i + 1) * IDX_DIM, :]
    wit_ref[...] = wi_ref[...].T[IDX_DIM:IDX_DIM + IDX_HEADS, :] * IDX_DIM ** -0.5

    def score_body(c, carry):
        off = pl.multiple_of(c * ck, ck)
        sc = _dot(ki_ref[pl.ds(off, ck), :], qit_ref[...])
        acc = jnp.zeros(chunk, F32)
        for h in range(IDX_HEADS):
            acc = acc + jnp.maximum(sc[:, h * LANES:(h + 1) * LANES], 0.0) * wit_ref[h:h + 1, :]
        ok = off + kpos_in <= qpos
        sc_ref[pl.ds(off, ck), :] = jnp.where(ok, acc, NO_SCORE)
        top, bottom, valid = carry

        def fold(x, op):
            parts = [x[r:r + stat_rows] for r in range(0, ck, stat_rows)]
            while len(parts) > 1:
                parts = [op(a, b) for a, b in zip(parts[::2], parts[1::2])]
            return parts[0]

        return (jnp.maximum(top, fold(jnp.where(ok, acc, NO_SCORE), jnp.maximum)),
                jnp.minimum(bottom, fold(jnp.where(ok, acc, -NO_SCORE), jnp.minimum)),
                valid + fold(jnp.where(ok, 1.0, 0.0), jnp.add))

    stats = lax.fori_loop(0, n_chunks, score_body,
                          (jnp.full((stat_rows, LANES), NO_SCORE, F32), jnp.full((stat_rows, LANES), -NO_SCORE, F32),
                           jnp.zeros((stat_rows, LANES), F32)))

    n_att = n_chunks * ck // ca

    def idx(c):
        return (pl.ds(pl.multiple_of(c * ca, ca), ca), slice(None))

    thr, cut, any_excess = _topk_threshold(sc_ref, idx, n_att, (ca, LANES), 0, float(topk), pos_bits, stats)

    @pl.when(any_excess)
    def _():
        def demote_body(c, carry):
            off = pl.multiple_of(c * ck, ck)
            sc = sc_ref[pl.ds(off, ck), :]
            sc_ref[pl.ds(off, ck), :] = jnp.where((sc == thr) & (off + kpos_in > cut), NO_SCORE, sc)
            return carry

        lax.fori_loop(0, n_chunks, demote_body, 0)

    thr_eff = jnp.broadcast_to(thr, (ca, LANES))

    m_ref[...] = jnp.full(m_ref.shape, NEG, F32)
    l_ref[...] = jnp.zeros(l_ref.shape, F32)
    acc_ref[...] = jnp.zeros(acc_ref.shape, F32)
    pair = 2 * HEAD_DIM
    n_pairs = AT_HEADS // 2

    def attn_body(c, carry):
        off = pl.multiple_of(c * ca, ca)
        bias = jnp.where(sc_ref[pl.ds(off, ca), :] >= thr_eff, 0.0, NEG)
        bias2 = jnp.concatenate([bias, bias], axis=1)
        scores = [_dot(k_ref[pl.ds(off, ca), ((2 * hp) // AT_GROUP) * HEAD_DIM:((2 * hp) // AT_GROUP + 1) * HEAD_DIM],
                       qt_ref[:, hp * pair:(hp + 1) * pair]) + bias2 for hp in range(n_pairs)]
        for hp, s in enumerate(scores):
            n = (2 * hp) // AT_GROUP
            sl = slice(hp * pair, (hp + 1) * pair)
            m_old = m_ref[:, sl]
            m_new = jnp.maximum(m_old, jnp.max(s, axis=0, keepdims=True))
            p = jnp.exp2(s - m_new).astype(BF16)
            alpha = jnp.exp2(m_old - m_new)
            m_ref[:, sl] = m_new
            pv = _dot(vt_ref[n, :, pl.ds(off, ca)], p)
            acc_ref[:, sl] = alpha * acc_ref[:, sl] + pv[:HEAD_DIM]
            l_ref[:, sl] = alpha * l_ref[:, sl] + pv[HEAD_DIM:HEAD_DIM + 1]
        return carry

    lax.fori_loop(0, n_att, attn_body, 0)
    for h in range(AT_HEADS):
        sl = slice(h * HEAD_DIM, (h + 1) * HEAD_DIM)
        o_ref[:, sl] = (acc_ref[:, sl] / l_ref[:, sl]).T * _silu(gc_ref[:, sl])


def _dsa_prompt(q_bf, qi_bf, wi, h3, k_bf, vt_bf, ki_bf, topk):
    b, s, _ = q_bf.shape
    nq = LANES
    ca = math.gcd(s, 4 * LANES)
    ck = ca
    assert s % ca == 0
    pos_bits = max(1, int(math.ceil(math.log2(s))))
    qblk = lambda w: pl.BlockSpec((None, nq, w), lambda bi, qi: (bi, qi, 0))
    full = lambda w: pl.BlockSpec((None, s, w), lambda bi, qi: (bi, 0, 0))
    return pl.pallas_call(
        functools.partial(_dsa_prompt_kernel, topk=topk, pos_bits=pos_bits, ck=ck, ca=ca),
        grid=(b, s // nq),
        in_specs=[qblk(AT_W), qblk(IDX_W), qblk(LANES),
                  pl.BlockSpec((None, nq, AT_W), lambda bi, qi: (bi, qi, O_GC // AT_W)),
                  full(KV_W),
                  pl.BlockSpec((None, AT_KV_HEADS, VT_ROWS, s), lambda bi, qi: (bi, 0, 0, 0)),
                  full(IDX_DIM)],
        out_specs=qblk(AT_W),
        out_shape=jax.ShapeDtypeStruct((b, s, AT_W), F32),
        scratch_shapes=[pltpu.VMEM((s, LANES), F32),
                        pltpu.VMEM((HEAD_DIM, AT_HEADS * nq), BF16),
                        pltpu.VMEM((IDX_DIM, IDX_HEADS * nq), BF16),
                        pltpu.VMEM((IDX_HEADS, nq), F32),
                        pltpu.VMEM((1, AT_HEADS * nq), F32),
                        pltpu.VMEM((1, AT_HEADS * nq), F32),
                        pltpu.VMEM((HEAD_DIM, AT_HEADS * nq), F32)],
        compiler_params=_cparams(("parallel", "arbitrary")),
        name="dsa_prompt",
    )(q_bf, qi_bf, wi, h3, k_bf, vt_bf, ki_bf)


def _dsa_sample_score_kernel(pt_ref, qi_ref, wi_ref, kin_ref, *rest, n_pp, n_steps, t_pad):
    page_refs = rest[:n_pp]
    sc_ref = rest[n_pp]
    step = pl.program_id(1)
    qi = qi_ref[...]
    wi = wi_ref[...]

    def head_sum(dots):
        sc = jnp.maximum(dots, 0.0) * wi
        acc = jnp.zeros((t_pad, LANES), F32)
        for h in range(IDX_HEADS):
            acc = acc + sc[h * t_pad:(h + 1) * t_pad, :]
        return acc

    def page_scores(ki_bf):
        return head_sum(_dot_nt(qi, ki_bf))

    for j in range(n_pp):
        off = pl.multiple_of((step * n_pp + j) * PAGE_SIZE, PAGE_SIZE)
        sc_ref[:, pl.ds(off, PAGE_SIZE)] = head_sum(_dot(qi, page_refs[j][...].astype(BF16)))

    @pl.when(step == n_steps - 1)
    def _():
        past = n_steps * n_pp * PAGE_SIZE
        shape = (t_pad, LANES)
        tok = lax.broadcasted_iota(I32, shape, 0)
        lane = lax.broadcasted_iota(I32, shape, 1)
        sc_ref[:, past:past + LANES] = jnp.where(lane <= tok, page_scores(kin_ref[...]), NO_SCORE)


def _dsa_sample_select_kernel(sc_ref, selx_ref, *, topk, pos_bits):
    rows, width = sc_ref.shape
    n_chunks = width // LANES
    chunk = (rows, LANES)
    xw = AT_KV_HEADS * LANES

    def idx(c):
        return (slice(None), pl.ds(pl.multiple_of(c * LANES, LANES), LANES))

    thr, cut, _ = _topk_threshold(sc_ref, idx, n_chunks, chunk, 1, float(topk), pos_bits)

    lane = lax.broadcasted_iota(I32, chunk, 1)
    src = lax.broadcasted_iota(I32, (LANES, xw), 0)
    dst = lax.broadcasted_iota(I32, (LANES, xw), 1)
    repeat = jnp.where((dst >= src * AT_KV_HEADS) & (dst < (src + 1) * AT_KV_HEADS), 1.0, 0.0).astype(BF16)

    def sel_body(c, carry):
        sel = _selected(sc_ref[idx(c)], thr, cut, c * LANES + lane)
        sel01 = jnp.where(sel, 1.0, 0.0).astype(BF16)
        selx_ref[:, pl.ds(pl.multiple_of(c * xw, xw), xw)] = _dot(sel01, repeat).astype(BF16)
        return carry

    lax.fori_loop(0, n_chunks, sel_body, 0)


def _dsa_sample_attn_kernel(pt_ref, q_ref, selx_ref, selxn_ref, kn_ref, vn_ref, gc_ref, *rest, n_pp, n_steps, n_tok):
    k_refs = rest[:n_pp]
    v_refs = rest[n_pp:2 * n_pp]
    o_ref, m_ref, l_ref, acc_ref = rest[2 * n_pp:2 * n_pp + 4]
    step = pl.program_id(1)
    rows = n_tok * AT_HEADS
    pw = AT_KV_HEADS * PAGE_SIZE
    shape = (rows, LANES)
    row = lax.broadcasted_iota(I32, (rows, pw), 0)
    col = lax.broadcasted_iota(I32, (rows, pw), 1)
    own_head = (lax.shift_right_logical(row, GROUP_SHIFT) & (AT_KV_HEADS - 1)) == (col & (AT_KV_HEADS - 1))
    q = q_ref[...]

    @pl.when(step == 0)
    def _():
        m_ref[...] = jnp.full(m_ref.shape, NEG, F32)
        l_ref[...] = jnp.zeros(l_ref.shape, F32)
        acc_ref[...] = jnp.zeros(acc_ref.shape, F32)

    def attend(blocks):
        parts = []
        for sx, k_ref, _ in blocks:
            sel = jnp.concatenate([jnp.broadcast_to(sx[t:t + 1], (AT_HEADS, pw)) for t in range(n_tok)], axis=0)
            parts.append(_dot_nt(q, k_ref[...].astype(BF16)) + jnp.where((sel > 0.5) & own_head, 0.0, NEG))
        s = jnp.concatenate(parts, axis=1)
        m_old = m_ref[...]
        m_new = jnp.maximum(m_old, jnp.broadcast_to(jnp.max(s, axis=1, keepdims=True), shape))
        p = jnp.exp2(s - m_new[:, 0:1])
        alpha = jnp.exp2(m_old - m_new)
        l_ref[...] = alpha * l_ref[...] + jnp.broadcast_to(jnp.sum(p, axis=1, keepdims=True), shape)
        acc = alpha * acc_ref[...]
        for j, (_, _, v_ref) in enumerate(blocks):
            acc = acc + _dot(p[:, j * pw:(j + 1) * pw].astype(BF16), v_ref[...].astype(BF16))
        acc_ref[...] = acc
        m_ref[...] = m_new

    selx = selx_ref[...].astype(F32)
    attend([(selx[:, j * pw:(j + 1) * pw], k_refs[j], v_refs[j]) for j in range(n_pp)])

    @pl.when(step == n_steps - 1)
    def _():
        attend([(selxn_ref[...].astype(F32), kn_ref, vn_ref)])
        o_ref[...] = acc_ref[...] / l_ref[...] * _silu(gc_ref[...])


def _dsa_sample(q_rows, qi_st, wi_st, gc_rows, k_new, v_new, ki_new, cache_k, cache_v, cache_ik, layer, page_table,
                topk, t_pad, n_tok):
    bd = q_rows.shape[0]
    n_pages = page_table.shape[1]
    n_pp = math.gcd(n_pages, PAGES_PER_STEP)
    n_steps = n_pages // n_pp
    past = n_pages * PAGE_SIZE
    width = past + LANES
    pos_bits = max(1, int(math.ceil(math.log2(width))))
    rows_i = IDX_HEADS * t_pad
    rows_a = AT_HEADS * n_tok
    pw = AT_KV_HEADS * PAGE_SIZE
    ck = cache_k.reshape(cache_k.shape[0], cache_k.shape[1], pw, HEAD_DIM)
    cv = cache_v.reshape(cache_v.shape[0], cache_v.shape[1], pw, HEAD_DIM)

    def page_spec(rows, cols, j):
        return pl.BlockSpec((None, None, rows, cols), lambda bi, si, pt: (layer, pt[bi, si * n_pp + j], 0, 0))

    per_b3 = lambda r, w: pl.BlockSpec((None, r, w), lambda bi, si, pt: (bi, 0, 0))
    scores = pl.pallas_call(
        functools.partial(_dsa_sample_score_kernel, n_pp=n_pp, n_steps=n_steps, t_pad=t_pad),
        grid_spec=pltpu.PrefetchScalarGridSpec(
            num_scalar_prefetch=1,
            grid=(bd, n_steps),
            in_specs=[per_b3(rows_i, IDX_DIM), per_b3(rows_i, LANES), per_b3(PAGE_SIZE, IDX_DIM)]
                     + [page_spec(IDX_DIM, PAGE_SIZE, j) for j in range(n_pp)],
            out_specs=per_b3(t_pad, width)),
        out_shape=jax.ShapeDtypeStruct((bd, t_pad, width), F32),
        compiler_params=_cparams(("parallel", "arbitrary")),
        name="dsa_sample_score",
    )(page_table, qi_st, wi_st, ki_new, *([jnp.swapaxes(cache_ik, 2, 3)] * n_pp))

    n_q = bd * n_tok
    rb = min(LANES, -(-n_q // SUBLANES) * SUBLANES)
    n_q_pad = -(-n_q // rb) * rb
    scores_q = jnp.pad(scores[:, :n_tok].reshape(n_q, width), ((0, n_q_pad - n_q), (0, 0)),
                       constant_values=NO_SCORE)
    selx = pl.pallas_call(
        functools.partial(_dsa_sample_select_kernel, topk=topk, pos_bits=pos_bits),
        grid=(n_q_pad // rb,),
        in_specs=[pl.BlockSpec((rb, width), lambda i: (i, 0))],
        out_specs=pl.BlockSpec((rb, AT_KV_HEADS * width), lambda i: (i, 0)),
        out_shape=jax.ShapeDtypeStruct((n_q_pad, AT_KV_HEADS * width), BF16),
        compiler_params=_cparams(("parallel",)),
        name="dsa_sample_select",
    )(scores_q)
    selx = selx[:n_q].reshape(bd, n_tok, AT_KV_HEADS * width)

    return pl.pallas_call(
        functools.partial(_dsa_sample_attn_kernel, n_pp=n_pp, n_steps=n_steps, n_tok=n_tok),
        grid_spec=pltpu.PrefetchScalarGridSpec(
            num_scalar_prefetch=1,
            grid=(bd, n_steps),
            in_specs=[per_b3(rows_a, HEAD_DIM),
                      pl.BlockSpec((None, n_tok, n_pp * pw), lambda bi, si, pt: (bi, 0, si)),
                      pl.BlockSpec((None, n_tok, pw), lambda bi, si, pt: (bi, 0, n_pages)),
                      per_b3(pw, HEAD_DIM), per_b3(pw, HEAD_DIM), per_b3(rows_a, HEAD_DIM)]
                     + [page_spec(pw, HEAD_DIM, j) for j in range(n_pp)]
                     + [page_spec(pw, HEAD_DIM, j) for j in range(n_pp)],
            out_specs=per_b3(rows_a, HEAD_DIM),
            scratch_shapes=[pltpu.VMEM((rows_a, LANES), F32),
                            pltpu.VMEM((rows_a, LANES), F32),
                            pltpu.VMEM((rows_a, HEAD_DIM), F32)]),
        out_shape=jax.ShapeDtypeStruct((bd, rows_a, HEAD_DIM), F32),
        compiler_params=_cparams(("parallel", "arbitrary")),
        name="dsa_sample_attn",
    )(page_table, q_rows, selx, selx, k_new, v_new, gc_rows, *([ck] * n_pp), *([cv] * n_pp))


def _row_tile(m):
    return math.gcd(m, 256)


def _even_layer(x3, mk, mv, c0, n0, m0, prm, t_valid):
    norm_g, w_in_bf, bif, hg, lng, lnb, ws, bs, w_out_bf, mqn = prm
    b, t, d = x3.shape
    L = math.gcd(t, LANES)
    x2 = x3.reshape(b * t, d)
    tm = _row_tile(b * t)
    h3 = _in_proj(x2, norm_g, w_in_bf, tm).reshape(b, t, E_COLS)
    m0b = jnp.broadcast_to(m0[:, :, None], (b, ML_HEADS, LANES))
    bs_t = jnp.pad(bs[:, :L].T, ((0, 0), (0, LANES - GM_GROUPS)))
    y, vbn, c, n, mb = _even_mix(h3, c0, n0, m0b, bif, hg.reshape(1, ML_W), lng.reshape(1, GM_W),
                                 lnb.reshape(1, GM_W), ws[:, :L, :L], bs_t, mk, mv, mqn.reshape(1, HEAD_DIM),
                                 L, t_valid)
    x_new = _out_proj(x2, [y.reshape(b * t, -1)], [w_out_bf], tm).reshape(b, t, d)
    return x_new, (c, n, mb[:, :, 0]), vbn


def _odd_common(x3, mk, mv, prm, tabs, v_transposed):
    norm_g, w_in_bf, qn, kn, w_out_c, w_out_x, mqn = prm
    b, t, d = x3.shape
    x2 = x3.reshape(b * t, d)
    tm = _row_tile(b * t)
    h3 = _in_proj(x2, norm_g, w_in_bf, tm).reshape(b, t, O_COLS)
    tq = math.gcd(t, 256)
    pre = _odd_pre(h3, tabs, qn.reshape(1, HEAD_DIM), kn.reshape(1, HEAD_DIM), mk, mv, mqn.reshape(1, HEAD_DIM), tq,
                   v_transposed)
    return x2, tm, h3, pre


def _finish_odd(x2, tm, yc2, yx, prm, shape):
    w_out_c, w_out_x = prm[4], prm[5]
    return _out_proj(x2, [yc2, yx.reshape(x2.shape[0], XA_W)], [w_out_c, w_out_x], tm).reshape(shape)


def kernel(x_prompt, x_sample, state_mlstm_C, state_mlstm_n, state_mlstm_m, cache_attn_k, cache_attn_v,
           cache_idx_k, cache_mem_k, cache_mem_v, page_table, mem_prompt, norm_even, w_in_even, b_if_even,
           mlstm_hnorm, gmlp_ln_g, gmlp_ln_b, gmlp_w, gmlp_b, w_out_even, norm_odd, w_in_odd, attn_qn, attn_kn,
           w_out_odd, w_mem_kv, mem_qn, mem_kn):
    B, S, D = x_prompt.shape
    Bd, T, _ = x_sample.shape
    depth = w_mem_kv.shape[0]
    n_mem = mem_prompt.shape[1]
    past = page_table.shape[1] * PAGE_SIZE
    assert S % LANES == 0 and T <= SAMPLE_PAD

    we = w_in_even
    we = jnp.concatenate([we[..., :2048], we[..., 2056:], we[..., 2048:2056],
                          jnp.zeros(we.shape[:2] + (LANES - 2 * ML_HEADS,), we.dtype)], axis=-1).astype(BF16)
    wo = w_in_odd
    wo = jnp.concatenate([wo[..., 0:1024], wo[..., 1536:2560], wo[..., 1024:1536], wo[..., 2560:3072],
                          wo[..., 3144:4168], wo[..., 3080:3144], wo[..., 3072:3080],
                          jnp.zeros(wo.shape[:2] + (LANES - IDX_DIM - IDX_HEADS,), wo.dtype)], axis=-1).astype(BF16)
    w_out_even_bf = w_out_even.astype(BF16)
    w_out_odd_bf = w_out_odd.astype(BF16)
    bif_pad = jnp.pad(b_if_even, ((0, 0), (0, LANES - 2 * ML_HEADS)))[:, None, :]

    mem_k_p, mem_v_p = _mem_kv(mem_prompt.reshape(B * n_mem, D), w_mem_kv.astype(BF16), mem_kn,
                               _row_tile(B * n_mem))
    mem_k_p = mem_k_p.reshape(depth, B, n_mem, XA_W)
    mem_v_p = mem_v_p.reshape(depth, B, n_mem, XA_W)
    cmk = cache_mem_k.reshape(depth, Bd, n_mem * XA_HEADS, HEAD_DIM)
    cmv = cache_mem_v.reshape(depth, Bd, n_mem * XA_HEADS, HEAD_DIM)

    tp = SAMPLE_PAD
    xp = x_prompt
    xs = jnp.pad(x_sample, ((0, 0), (0, tp - T), (0, 0)))
    tabs_p = _rope_tables(jnp.arange(S))
    tabs_s = _rope_tables(past + jnp.arange(tp))
    zero_c = jnp.zeros((B, ML_HEADS, HEAD_DIM, HEAD_DIM), F32)
    zero_n = jnp.zeros((B, ML_HEADS, HEAD_DIM), F32)
    zero_m = jnp.zeros((B, ML_HEADS), F32)
    topk_p = min(TOPK_MAX, S // 4)
    topk_s = min(TOPK_MAX, (past + T) // 4)

    mlC_p, mln_p, mlm_p, mlC_s, mln_s, mlm_s, gv_s = [], [], [], [], [], [], []
    ak_p, av_p, ik_p, ak_s, av_s, ik_s = [], [], [], [], [], []
    for l in range(depth):
        if l % 2 == 0:
            e = l // 2
            prm = (norm_even[e], we[e], bif_pad[e], mlstm_hnorm[e], gmlp_ln_g[e], gmlp_ln_b[e], gmlp_w[e], gmlp_b[e],
                   w_out_even_bf[e], mem_qn[l])
            xp, (c, n, m), _ = _even_layer(xp, mem_k_p[l], mem_v_p[l], zero_c, zero_n, zero_m, prm, LANES)
            mlC_p.append(c); mln_p.append(n); mlm_p.append(m)
            xs, (c, n, m), vrows = _even_layer(xs, cmk[l], cmv[l], state_mlstm_C[e], state_mlstm_n[e],
                                               state_mlstm_m[e], prm, T)
            mlC_s.append(c); mln_s.append(n); mlm_s.append(m)
            gv_s.append(vrows[:, :T])
        else:
            o = l // 2
            prm = (norm_odd[o], wo[o], attn_qn[o], attn_kn[o], w_out_odd_bf[o, :AT_W], w_out_odd_bf[o, AT_W:],
                   mem_qn[l])
            x2, tm, h3, pre = _odd_common(xp, mem_k_p[l], mem_v_p[l], prm, tabs_p, True)
            q_bf, k, v, k_bf, vt_bf, qi_bf, ki, ki_bf, wi, yx = pre
            yc = _dsa_prompt(q_bf, qi_bf, wi, h3, k_bf, vt_bf, ki_bf, topk_p)
            xp = _finish_odd(x2, tm, yc.reshape(B * S, AT_W), yx, prm, xp.shape)
            ak_p.append(k.reshape(B, S, AT_KV_HEADS, HEAD_DIM))
            av_p.append(v.reshape(B, S, AT_KV_HEADS, HEAD_DIM))
            ik_p.append(ki)
            x2, tm, h3, pre = _odd_common(xs, cmk[l], cmv[l], prm, tabs_s, False)
            q_bf, k, v, k_bf, v_bf, qi_bf, ki, ki_bf, wi, yx = pre
            q_rows = q_bf[:, :T].reshape(Bd, T * AT_HEADS, HEAD_DIM)
            gc_rows = h3[:, :T, O_GC:O_GC + AT_W].reshape(Bd, T * AT_HEADS, HEAD_DIM)
            qi_st = qi_bf.reshape(Bd, tp, IDX_HEADS, IDX_DIM).transpose(0, 2, 1, 3).reshape(
                Bd, IDX_HEADS * tp, IDX_DIM)
            wi_h = wi[:, :, IDX_DIM:IDX_DIM + IDX_HEADS] * IDX_DIM ** -0.5
            wi_st = jnp.broadcast_to(wi_h.transpose(0, 2, 1).reshape(Bd, IDX_HEADS * tp, 1),
                                     (Bd, IDX_HEADS * tp, LANES))
            kv_page = lambda a: jnp.pad(a.reshape(Bd, tp * AT_KV_HEADS, HEAD_DIM),
                                        ((0, 0), (0, AT_KV_HEADS * (PAGE_SIZE - tp)), (0, 0)))
            ki_page = jnp.pad(ki_bf, ((0, 0), (0, PAGE_SIZE - tp), (0, 0)))
            y_rows = _dsa_sample(q_rows, qi_st, wi_st, gc_rows, kv_page(k), kv_page(v), ki_page,
                                 cache_attn_k, cache_attn_v, cache_idx_k, o, page_table, topk_s, tp, T)
            yc = jnp.pad(y_rows.reshape(Bd, T, AT_W), ((0, 0), (0, tp - T), (0, 0))).reshape(Bd * tp, AT_W)
            xs = _finish_odd(x2, tm, yc, yx, prm, xs.shape)
            ak_s.append(k[:, :T].reshape(Bd, T, AT_KV_HEADS, HEAD_DIM))
            av_s.append(v[:, :T].reshape(Bd, T, AT_KV_HEADS, HEAD_DIM))
            ik_s.append(ki[:, :T])
    return (xp, xs[:, :T],
            jnp.stack(mlC_p), jnp.stack(mln_p), jnp.stack(mlm_p),
            jnp.stack(mlC_s), jnp.stack(mln_s), jnp.stack(mlm_s),
            jnp.stack(gv_s),
            jnp.stack(ak_p), jnp.stack(av_p), jnp.stack(ik_p),
            jnp.stack(ak_s), jnp.stack(av_s), jnp.stack(ik_s),
            mem_k_p.reshape(depth, B, n_mem, XA_HEADS, HEAD_DIM),
            mem_v_p.reshape(depth, B, n_mem, XA_HEADS, HEAD_DIM))
```

```python
import functools
import math

import jax
import jax.numpy as jnp
from jax import lax
from jax.experimental import pallas as pl
from jax.experimental.pallas import tpu as pltpu

F32 = jnp.float32
BF16 = jnp.bfloat16
I32 = jnp.int32

HEAD_DIM = 128
ML_HEADS = 4
GM_GROUPS = 4
AT_HEADS = 8
AT_KV_HEADS = 2
AT_GROUP = AT_HEADS // AT_KV_HEADS
IDX_HEADS = 8
IDX_DIM = 64
XA_HEADS = 4
TOPK_MAX = 256
PAGE_SIZE = 128
ROPE_THETA = 500000.0
ROPE_FRAC = 4
NORM_EPS = 1e-6

ML_W = ML_HEADS * HEAD_DIM
GM_W = GM_GROUPS * HEAD_DIM
XA_W = XA_HEADS * HEAD_DIM
AT_W = AT_HEADS * HEAD_DIM
KV_W = AT_KV_HEADS * HEAD_DIM
IDX_W = IDX_HEADS * IDX_DIM

LANES = 128
SUBLANES = 8
VMEM_LIMIT = 56 * 1024 * 1024
NEG = -1e30
NO_SCORE = -3.0e38
BISECT_STEPS = 16
SAMPLE_PAD = SUBLANES
BF16_SUBLANES = 16
VT_ROWS = HEAD_DIM + BF16_SUBLANES
Q_SCALE = HEAD_DIM ** -0.5 * math.log2(math.e)
PAGES_PER_STEP = 16
COUNT_ACC_TILES = 4
GROUP_SHIFT = AT_GROUP.bit_length() - 1
assert AT_GROUP == 1 << GROUP_SHIFT and AT_KV_HEADS & (AT_KV_HEADS - 1) == 0 and XA_HEADS & (XA_HEADS - 1) == 0

E_Q, E_K, E_V, E_O, E_GA, E_U, E_VB, E_GB, E_XQ, E_GX, E_IF = (
    0, 512, 1024, 1536, 2048, 2560, 3072, 3584, 4096, 4608, 5120)
E_MAIN = 5120
E_COLS = E_MAIN + LANES
O_Q, O_GC, O_K, O_V, O_QI, O_XQ, O_GX, O_KI = 0, 1024, 2048, 2304, 2560, 3072, 3584, 4096
O_MAIN = 4096
O_COLS = O_MAIN + LANES


def _cparams(sem):
    return pltpu.CompilerParams(dimension_semantics=sem, vmem_limit_bytes=VMEM_LIMIT)


def _sigmoid(x):
    return 1.0 / (1.0 + jnp.exp(-x))


def _silu(x):
    return x * _sigmoid(x)


def _rms_rows(x, g):
    return x * lax.rsqrt(jnp.mean(x * x, axis=-1, keepdims=True) + NORM_EPS) * g


def _dot_nt(a, b):
    return lax.dot_general(a, b, (((1,), (1,)), ((), ())), preferred_element_type=F32)


def _dot_tn(a, b):
    return lax.dot_general(a, b, (((0,), (0,)), ((), ())), preferred_element_type=F32)


def _dot(a, b):
    return jnp.dot(a, b, preferred_element_type=F32)


def _in_proj_kernel(x_ref, g_ref, w_ref, o_ref, *, col_chunk):
    xn = _rms_rows(x_ref[...], g_ref[...]).astype(BF16)
    n = o_ref.shape[-1]
    for c0 in range(0, n, col_chunk):
        c1 = min(c0 + col_chunk, n)
        o_ref[:, c0:c1] = _dot(xn, w_ref[:, c0:c1])


def _in_proj(x2, g, w_bf, tm):
    m, d = x2.shape
    n = w_bf.shape[1]
    return pl.pallas_call(
        functools.partial(_in_proj_kernel, col_chunk=512),
        grid=(m // tm,),
        in_specs=[pl.BlockSpec((tm, d), lambda i: (i, 0)),
                  pl.BlockSpec((1, d), lambda i: (0, 0)),
                  pl.BlockSpec((d, n), lambda i: (0, 0))],
        out_specs=pl.BlockSpec((tm, n), lambda i: (i, 0)),
        out_shape=jax.ShapeDtypeStruct((m, n), F32),
        compiler_params=_cparams(("parallel",)),
        name="in_proj",
    )(x2, g.reshape(1, d), w_bf)


def _out_proj_kernel(*refs, n_in):
    x_ref = refs[0]
    y_refs = refs[1:1 + n_in]
    w_refs = refs[1 + n_in:1 + 2 * n_in]
    o_ref = refs[1 + 2 * n_in]
    acc = x_ref[...]
    for y_ref, w_ref in zip(y_refs, w_refs):
        acc = acc + _dot(y_ref[...].astype(BF16), w_ref[...])
    o_ref[...] = acc


def _out_proj(x2, ys, ws, tm):
    m, d = x2.shape
    n_in = len(ys)
    in_specs = [pl.BlockSpec((tm, d), lambda i: (i, 0))]
    in_specs += [pl.BlockSpec((tm, y.shape[1]), lambda i: (i, 0)) for y in ys]
    in_specs += [pl.BlockSpec(w.shape, lambda i: (0, 0)) for w in ws]
    return pl.pallas_call(
        functools.partial(_out_proj_kernel, n_in=n_in),
        grid=(m // tm,),
        in_specs=in_specs,
        out_specs=pl.BlockSpec((tm, d), lambda i: (i, 0)),
        out_shape=jax.ShapeDtypeStruct((m, d), F32),
        compiler_params=_cparams(("parallel",)),
        name="out_proj",
    )(x2, *ys, *ws)


def _mem_kv_kernel(mem_ref, w_ref, kn_ref, k_ref, v_ref):
    kv = _dot(mem_ref[...].astype(BF16), w_ref[...])
    kn = kn_ref[...]
    for h in range(XA_HEADS):
        sl = slice(h * HEAD_DIM, (h + 1) * HEAD_DIM)
        k_ref[:, sl] = _rms_rows(kv[:, sl], kn)
    v_ref[...] = kv[:, XA_W:]


def _mem_kv(mem2, w_kv_bf, mem_kn, tm):
    m, d = mem2.shape
    depth = w_kv_bf.shape[0]
    out = jax.ShapeDtypeStruct((depth, m, XA_W), F32)
    return pl.pallas_call(
        _mem_kv_kernel,
        grid=(depth, m // tm),
        in_specs=[pl.BlockSpec((tm, d), lambda l, i: (i, 0)),
                  pl.BlockSpec((None, d, 2 * XA_W), lambda l, i: (l, 0, 0)),
                  pl.BlockSpec((None, 1, HEAD_DIM), lambda l, i: (l, 0, 0))],
        out_specs=[pl.BlockSpec((None, tm, XA_W), lambda l, i: (l, i, 0)),
                   pl.BlockSpec((None, tm, XA_W), lambda l, i: (l, i, 0))],
        out_shape=[out, out],
        compiler_params=_cparams(("parallel", "parallel")),
        name="mem_kv",
    )(mem2, w_kv_bf, mem_kn.reshape(depth, 1, HEAD_DIM))


def _mem_attn_rows(xq, gx, mk_ref, mv_ref, qn):
    n_q = xq.shape[0]
    q = jnp.concatenate([_rms_rows(xq[:, h * HEAD_DIM:(h + 1) * HEAD_DIM], qn) for h in range(XA_HEADS)],
                        axis=0).astype(BF16)
    s = _dot_nt(q, mk_ref[...].astype(BF16)) * HEAD_DIM ** -0.5
    col_head = lax.broadcasted_iota(I32, (n_q, s.shape[1]), 1) & (XA_HEADS - 1)
    s = s + jnp.concatenate([jnp.where(col_head == h, 0.0, NEG) for h in range(XA_HEADS)], axis=0)
    s = s - jnp.max(s, axis=-1, keepdims=True)
    p = jnp.exp(s)
    p = p / jnp.sum(p, axis=-1, keepdims=True)
    o = _dot(p.astype(BF16), mv_ref[...].astype(BF16))
    return [o[h * n_q:(h + 1) * n_q] * _silu(gx[:, h * HEAD_DIM:(h + 1) * HEAD_DIM]) for h in range(XA_HEADS)]


def _mem_attn_heads(xq, gx, mk_ref, mv_ref, qn):
    if mk_ref.shape[-1] == HEAD_DIM:
        return _mem_attn_rows(xq, gx, mk_ref, mv_ref, qn)
    outs = []
    for h in range(XA_HEADS):
        sl = slice(h * HEAD_DIM, (h + 1) * HEAD_DIM)
        q = _rms_rows(xq[:, sl], qn).astype(BF16)
        s = _dot_nt(q, mk_ref[:, sl].astype(BF16)) * HEAD_DIM ** -0.5
        s = s - jnp.max(s, axis=-1, keepdims=True)
        p = jnp.exp(s)
        p = p / jnp.sum(p, axis=-1, keepdims=True)
        o = _dot(p.astype(BF16), mv_ref[:, sl].astype(BF16))
        outs.append(o * _silu(gx[:, sl]))
    return outs


def _even_mix_kernel(h_ref, if_ref, c0_ref, n0_ref, m0_ref, bif_ref, hg_ref, lng_ref, lnb_ref,
                     ws_ref, bs_ref, mk_ref, mv_ref, qn_ref,
                     y_ref, vbn_ref, c_ref, n_ref, m_ref, *, t_valid):
    L = h_ref.shape[0]
    ci = pl.program_id(1)

    @pl.when(ci == 0)
    def _():
        c_ref[...] = c0_ref[...]
        n_ref[...] = n0_ref[...]
        m_ref[...] = m0_ref[...]

    row = lax.broadcasted_iota(I32, (L, L), 0)
    col = lax.broadcasted_iota(I32, (L, L), 1)
    eye = row == col
    row_le_col = row <= col
    col_le_row = col <= row
    tcol = lax.broadcasted_iota(I32, (L, 1), 0)

    gates = if_ref[...] + bif_ref[...]
    log_f_all = jnp.minimum(gates, 0.0) - jnp.log(1.0 + jnp.exp(-jnp.abs(gates)))

    for h in range(ML_HEADS):
        sl = slice(h * HEAD_DIM, (h + 1) * HEAD_DIM)
        li_col = gates[:, h:h + 1]
        lf_col = log_f_all[:, ML_HEADS + h:ML_HEADS + h + 1]
        if t_valid < L:
            li_col = jnp.where(tcol < t_valid, li_col, NEG)
            lf_col = jnp.where(tcol < t_valid, lf_col, 0.0)
        li_b = jnp.broadcast_to(li_col, (L, L))
        lf_b = jnp.broadcast_to(lf_col, (L, L))
        li_row = jnp.sum(jnp.where(eye, li_b, 0.0), axis=0, keepdims=True)
        lf_row = jnp.sum(jnp.where(eye, lf_b, 0.0), axis=0, keepdims=True)
        b_row = jnp.sum(jnp.where(row_le_col, lf_b, 0.0), axis=0, keepdims=True)
        b_col = jnp.sum(jnp.where(col_le_row, jnp.broadcast_to(lf_row, (L, L)), 0.0),
                        axis=1, keepdims=True)
        m_prev = m_ref[h:h + 1, 0:1]
        n_prev = n_ref[h:h + 1, :]
        c_prev = c_ref[h]

        dlog = jnp.where(col_le_row, b_col - b_row + li_row, NEG)
        inter = b_col + m_prev
        m_t = jnp.maximum(inter, jnp.max(dlog, axis=1, keepdims=True))
        w_intra = jnp.exp(dlog - m_t)
        w_inter = jnp.exp(inter - m_t)
        q = h_ref[:, E_Q + h * HEAD_DIM:E_Q + (h + 1) * HEAD_DIM]
        k = h_ref[:, E_K + h * HEAD_DIM:E_K + (h + 1) * HEAD_DIM] * HEAD_DIM ** -0.5
        v = h_ref[:, E_V + h * HEAD_DIM:E_V + (h + 1) * HEAD_DIM]
        q_bf = q.astype(BF16)
        k_bf = k.astype(BF16)
        qk = _dot_nt(q_bf, k_bf) * w_intra
        num = _dot(qk.astype(BF16), v.astype(BF16)) + w_inter * _dot(q_bf, c_prev.astype(BF16))
        den = jnp.sum(qk, axis=1, keepdims=True) + w_inter * jnp.sum(q * n_prev, axis=1, keepdims=True)
        hh = num / jnp.maximum(jnp.abs(den), jnp.exp(-m_t))

        b_end = b_col[L - 1:L, :]
        g = b_end - b_col + li_col
        m_new = jnp.maximum(b_end + m_prev, jnp.max(g, axis=0, keepdims=True))
        a_prev = jnp.exp(b_end + m_prev - m_new)
        a_tok = jnp.exp(g - m_new)
        c_ref[h] = a_prev * c_prev + _dot_tn(k_bf, (a_tok * v).astype(BF16))
        n_ref[h:h + 1, :] = a_prev * n_prev + jnp.sum(a_tok * k, axis=0, keepdims=True)
        m_ref[h:h + 1, :] = jnp.broadcast_to(m_new, (1, LANES))

        ha = _rms_rows(hh, hg_ref[:, sl])
        y_ref[:, sl] = ha * _sigmoid(h_ref[:, E_O + h * HEAD_DIM:E_O + (h + 1) * HEAD_DIM]) * _silu(
            h_ref[:, E_GA + h * HEAD_DIM:E_GA + (h + 1) * HEAD_DIM])

    vb = h_ref[:, E_VB:E_VB + GM_W]
    vc = vb - jnp.mean(vb, axis=-1, keepdims=True)
    var = jnp.mean(vc * vc, axis=-1, keepdims=True)
    vbn = vc * lax.rsqrt(var + NORM_EPS) * lng_ref[...] + lnb_ref[...]
    vbn_ref[...] = vbn
    for g_i in range(GM_GROUPS):
        sl = slice(g_i * HEAD_DIM, (g_i + 1) * HEAD_DIM)
        w = jnp.where(col_le_row, ws_ref[g_i], 0.0).astype(BF16)
        mixed = _dot(w, vbn[:, sl].astype(BF16)) + bs_ref[:, g_i:g_i + 1]
        y_ref[:, ML_W + g_i * HEAD_DIM:ML_W + (g_i + 1) * HEAD_DIM] = (
            h_ref[:, E_U + g_i * HEAD_DIM:E_U + (g_i + 1) * HEAD_DIM] * mixed
            * _silu(h_ref[:, E_GB + g_i * HEAD_DIM:E_GB + (g_i + 1) * HEAD_DIM]))

    outs = _mem_attn_heads(h_ref[:, E_XQ:E_XQ + XA_W], h_ref[:, E_GX:E_GX + XA_W], mk_ref, mv_ref, qn_ref[...])
    for h, o in enumerate(outs):
        y_ref[:, ML_W + GM_W + h * HEAD_DIM:ML_W + GM_W + (h + 1) * HEAD_DIM] = o


def _even_mix(h3, c0, n0, m0b, bif, hg, lng, lnb, ws, bs_t, mk, mv, qn, L, t_valid):
    b, t, _ = h3.shape
    nc = t // L
    full2 = lambda bi, ci: (0, 0)
    return pl.pallas_call(
        functools.partial(_even_mix_kernel, t_valid=t_valid),
        grid=(b, nc),
        in_specs=[pl.BlockSpec((None, L, E_MAIN), lambda bi, ci: (bi, ci, 0)),
                  pl.BlockSpec((None, L, LANES), lambda bi, ci: (bi, ci, E_MAIN // LANES)),
                  pl.BlockSpec((None, ML_HEADS, HEAD_DIM, HEAD_DIM), lambda bi, ci: (bi, 0, 0, 0)),
                  pl.BlockSpec((None, ML_HEADS, HEAD_DIM), lambda bi, ci: (bi, 0, 0)),
                  pl.BlockSpec((None, ML_HEADS, LANES), lambda bi, ci: (bi, 0, 0)),
                  pl.BlockSpec((1, LANES), full2),
                  pl.BlockSpec((1, ML_W), full2),
                  pl.BlockSpec((1, GM_W), full2),
                  pl.BlockSpec((1, GM_W), full2),
                  pl.BlockSpec((GM_GROUPS, L, L), lambda bi, ci: (0, 0, 0)),
                  pl.BlockSpec((L, LANES), full2),
                  pl.BlockSpec((None,) + mk.shape[1:], lambda bi, ci: (bi, 0, 0)),
                  pl.BlockSpec((None,) + mv.shape[1:], lambda bi, ci: (bi, 0, 0)),
                  pl.BlockSpec((1, HEAD_DIM), full2)],
        out_specs=[pl.BlockSpec((None, L, ML_W + GM_W + XA_W), lambda bi, ci: (bi, ci, 0)),
                   pl.BlockSpec((None, L, GM_W), lambda bi, ci: (bi, ci, 0)),
                   pl.BlockSpec((None, ML_HEADS, HEAD_DIM, HEAD_DIM), lambda bi, ci: (bi, 0, 0, 0)),
                   pl.BlockSpec((None, ML_HEADS, HEAD_DIM), lambda bi, ci: (bi, 0, 0)),
                   pl.BlockSpec((None, ML_HEADS, LANES), lambda bi, ci: (bi, 0, 0))],
        out_shape=[jax.ShapeDtypeStruct((b, t, ML_W + GM_W + XA_W), F32),
                   jax.ShapeDtypeStruct((b, t, GM_W), F32),
                   jax.ShapeDtypeStruct((b, ML_HEADS, HEAD_DIM, HEAD_DIM), F32),
                   jax.ShapeDtypeStruct((b, ML_HEADS, HEAD_DIM), F32),
                   jax.ShapeDtypeStruct((b, ML_HEADS, LANES), F32)],
        compiler_params=_cparams(("parallel", "arbitrary")),
        name="even_mix",
    )(h3, h3, c0, n0, m0b, bif, hg, lng, lnb, ws, bs_t, mk, mv, qn)


def _rope_lanes(x, cos, sin_lo, sin_hi, half):
    n = x.shape[-1]
    return x * cos + pltpu.roll(x, n - half, 1) * sin_lo + pltpu.roll(x, half, 1) * sin_hi


def _odd_pre_kernel(h_ref, kiwi_ref, ca_ref, sa1_ref, sa2_ref, ci_ref, si1_ref, si2_ref,
                    qn_ref, kn_ref, mk_ref, mv_ref, mqn_ref,
                    q_ref, k_ref, v_ref, kbf_ref, vbf_ref, qi_ref, ki_ref, kibf_ref, wi_ref, yx_ref, *, v_transposed):
    ca, sa1, sa2 = ca_ref[...], sa1_ref[...], sa2_ref[...]
    ci, si1, si2 = ci_ref[...], si1_ref[...], si2_ref[...]
    a_half = HEAD_DIM // ROPE_FRAC // 2
    i_half = IDX_DIM // ROPE_FRAC // 2
    for h in range(AT_HEADS):
        sl = slice(O_Q + h * HEAD_DIM, O_Q + (h + 1) * HEAD_DIM)
        qh = _rope_lanes(_rms_rows(h_ref[:, sl], qn_ref[...]), ca, sa1, sa2, a_half)
        q_ref[:, h * HEAD_DIM:(h + 1) * HEAD_DIM] = (qh * Q_SCALE).astype(BF16)
    for h in range(AT_KV_HEADS):
        sl = slice(O_K + h * HEAD_DIM, O_K + (h + 1) * HEAD_DIM)
        kh = _rope_lanes(_rms_rows(h_ref[:, sl], kn_ref[...]), ca, sa1, sa2, a_half)
        k_ref[:, h * HEAD_DIM:(h + 1) * HEAD_DIM] = kh
        kbf_ref[:, h * HEAD_DIM:(h + 1) * HEAD_DIM] = kh.astype(BF16)
    v = h_ref[:, O_V:O_V + KV_W]
    v_ref[...] = v
    if v_transposed:
        tail_row = lax.broadcasted_iota(I32, (VT_ROWS - HEAD_DIM, v.shape[0]), 0)
        tail = jnp.where(tail_row == 0, 1.0, 0.0).astype(BF16)
        for n in range(AT_KV_HEADS):
            vbf_ref[n, :HEAD_DIM, :] = v[:, n * HEAD_DIM:(n + 1) * HEAD_DIM].T.astype(BF16)
            vbf_ref[n, HEAD_DIM:, :] = tail
    else:
        vbf_ref[...] = v.astype(BF16)
    for j in range(IDX_W // LANES):
        sl = slice(O_QI + j * LANES, O_QI + (j + 1) * LANES)
        qi_ref[:, j * LANES:(j + 1) * LANES] = _rope_lanes(h_ref[:, sl], ci, si1, si2, i_half).astype(BF16)
    kiwi = kiwi_ref[...]
    ki = _rope_lanes(kiwi, ci, si1, si2, i_half)[:, :IDX_DIM]
    ki_ref[...] = ki
    kibf_ref[...] = ki.astype(BF16)
    lane = lax.broadcasted_iota(I32, kiwi.shape, 1)
    wsel = (lane >= IDX_DIM) & (lane < IDX_DIM + IDX_HEADS)
    wi_ref[...] = jnp.where(wsel, kiwi * IDX_HEADS ** -0.5, 0.0)
    outs = _mem_attn_heads(h_ref[:, O_XQ:O_XQ + XA_W], h_ref[:, O_GX:O_GX + XA_W], mk_ref, mv_ref, mqn_ref[...])
    for h, o in enumerate(outs):
        yx_ref[:, h * HEAD_DIM:(h + 1) * HEAD_DIM] = o


def _odd_pre(h3, tabs, qn, kn, mk, mv, mqn, tq, v_transposed):
    b, t, _ = h3.shape
    row_blk = lambda w: pl.BlockSpec((None, tq, w), lambda bi, ti: (bi, ti, 0))
    tab = pl.BlockSpec((tq, LANES), lambda bi, ti: (ti, 0))
    vec = pl.BlockSpec((1, HEAD_DIM), lambda bi, ti: (0, 0))
    mem = pl.BlockSpec((None,) + mk.shape[1:], lambda bi, ti: (bi, 0, 0))
    sds = lambda w, dt: jax.ShapeDtypeStruct((b, t, w), dt)
    if v_transposed:
        vbf_spec = pl.BlockSpec((None, AT_KV_HEADS, VT_ROWS, tq), lambda bi, ti: (bi, 0, 0, ti))
        vbf_shape = jax.ShapeDtypeStruct((b, AT_KV_HEADS, VT_ROWS, t), BF16)
    else:
        vbf_spec, vbf_shape = row_blk(KV_W), sds(KV_W, BF16)
    return pl.pallas_call(
        functools.partial(_odd_pre_kernel, v_transposed=v_transposed),
        grid=(b, t // tq),
        in_specs=[row_blk(O_MAIN),
                  pl.BlockSpec((None, tq, LANES), lambda bi, ti: (bi, ti, O_MAIN // LANES)),
                  tab, tab, tab, tab, tab, tab, vec, vec, mem, mem, vec],
        out_specs=[row_blk(AT_W), row_blk(KV_W), row_blk(KV_W), row_blk(KV_W), vbf_spec,
                   row_blk(IDX_W), row_blk(IDX_DIM), row_blk(IDX_DIM), row_blk(LANES), row_blk(XA_W)],
        out_shape=[sds(AT_W, BF16), sds(KV_W, F32), sds(KV_W, F32), sds(KV_W, BF16), vbf_shape,
                   sds(IDX_W, BF16), sds(IDX_DIM, F32), sds(IDX_DIM, BF16), sds(LANES, F32), sds(XA_W, F32)],
        compiler_params=_cparams(("parallel", "parallel")),
        name="odd_pre",
    )(h3, h3, *tabs, qn, kn, mk, mv, mqn)


def _rope_tables(pos):
    pos = pos.astype(F32)[:, None]

    def build(width, reps):
        rd = width // ROPE_FRAC
        half = rd // 2
        inv = ROPE_THETA ** (-jnp.arange(half, dtype=F32) / half)
        ang = pos * inv[None, :]
        cos, sin = jnp.cos(ang), jnp.sin(ang)
        ones = jnp.ones((pos.shape[0], width - rd), F32)
        zeros_h = jnp.zeros((pos.shape[0], half), F32)
        zeros_r = jnp.zeros((pos.shape[0], width - rd), F32)
        c = jnp.concatenate([cos, cos, ones], axis=1)
        s_lo = jnp.concatenate([-sin, zeros_h, zeros_r], axis=1)
        s_hi = jnp.concatenate([zeros_h, sin, zeros_r], axis=1)
        return [jnp.tile(a, (1, reps)) for a in (c, s_lo, s_hi)]

    return build(HEAD_DIM, 1) + build(IDX_DIM, LANES // IDX_DIM)


def _topk_threshold(score_ref, idx, n_chunks, chunk_shape, key_axis, k, pos_bits, stats=None):
    red_shape = tuple(1 if a == key_axis else d for a, d in enumerate(chunk_shape))
    pos_in_chunk = lax.broadcasted_iota(I32, chunk_shape, key_axis)
    acc_rows = min(chunk_shape[0], COUNT_ACC_TILES * SUBLANES) if key_axis == 0 else chunk_shape[0]
    acc_shape = (acc_rows, chunk_shape[1])
    zeros = jnp.zeros(acc_shape, F32)

    def folded(x, op=jnp.add):
        parts = [x[r:r + acc_rows] for r in range(0, chunk_shape[0], acc_rows)]
        while len(parts) > 1:
            parts = [op(a, b) for a, b in zip(parts[::2], parts[1::2])] + ([parts[-1]] if len(parts) % 2 else [])
        return parts[0]

    def count(pred):
        def body(c, acc):
            return acc + folded(jnp.where(pred(score_ref[idx(c)], c), 1.0, 0.0))

        return jnp.sum(lax.fori_loop(0, n_chunks, body, zeros), axis=key_axis, keepdims=True)

    def count_ge(t):
        t_b = jnp.broadcast_to(t, chunk_shape)
        return count(lambda s, c: s >= t_b)

    def range_body(c, carry):
        top, bottom, valid = carry
        s = score_ref[idx(c)]
        ok = s > NO_SCORE
        return (jnp.maximum(top, folded(s, jnp.maximum)),
                jnp.minimum(bottom, folded(jnp.where(ok, s, -NO_SCORE), jnp.minimum)),
                valid + folded(jnp.where(ok, 1.0, 0.0)))

    if stats is None:
        stats = lax.fori_loop(0, n_chunks, range_body,
                              (jnp.full(acc_shape, NO_SCORE, F32), jnp.full(acc_shape, -NO_SCORE, F32), zeros))
    top, bottom, valid = stats
    top = jnp.max(top, axis=key_axis, keepdims=True)
    bottom = jnp.min(bottom, axis=key_axis, keepdims=True)
    rank = jnp.minimum(jnp.sum(valid, axis=key_axis, keepdims=True), k)

    def bisect_step(i, carry):
        lo, mid_hi, hi = carry
        mid = 0.5 * (lo + mid_hi)
        enough = count_ge(mid) >= rank
        return jnp.where(enough, mid, lo), jnp.where(enough, mid_hi, mid), jnp.where(enough, hi, mid)

    _, _, hi = lax.fori_loop(0, BISECT_STEPS, bisect_step, (bottom, top, jnp.full(red_shape, -NO_SCORE, F32)))

    def walk_cond(carry):
        return jnp.max(carry[2]) > 0.0

    def walk_body(carry):
        hi, thr, todo = carry
        hi_b = jnp.broadcast_to(hi, chunk_shape)

        def body(c, acc):
            s = score_ref[idx(c)]
            return jnp.maximum(acc, folded(jnp.where(s < hi_b, s, NO_SCORE), jnp.maximum))

        cand = jnp.max(lax.fori_loop(0, n_chunks, body, jnp.full(acc_shape, NO_SCORE, F32)),
                       axis=key_axis, keepdims=True)
        found = count_ge(cand) >= rank
        open_ = todo > 0.0
        return (jnp.where(open_, cand, hi), jnp.where(open_, cand, thr),
                jnp.where(open_ & jnp.logical_not(found), 1.0, 0.0))

    _, thr, _ = lax.while_loop(walk_cond, walk_body, (hi, bottom, jnp.ones(red_shape, F32)))
    thr_b = jnp.broadcast_to(thr, chunk_shape)

    def count_body(c, carry):
        gt, eq = carry
        s = score_ref[idx(c)]
        return (gt + folded(jnp.where(s > thr_b, 1.0, 0.0)), eq + folded(jnp.where(s == thr_b, 1.0, 0.0)))

    gt, eq = lax.fori_loop(0, n_chunks, count_body, (zeros, zeros))
    need = rank - jnp.sum(gt, axis=key_axis, keepdims=True)
    excess = jnp.sum(eq, axis=key_axis, keepdims=True) - need

    def pos_step(i, cut):
        cand = cut + lax.shift_left(jnp.int32(1), pos_bits - 1 - i)
        cand_b = jnp.broadcast_to(cand, chunk_shape)
        cnt = count(lambda s, c: (s == thr_b) & (c * chunk_shape[key_axis] + pos_in_chunk < cand_b))
        return jnp.where(cnt < need, cand, cut)

    any_excess = jnp.max(excess) > 0.0
    cut = lax.cond(any_excess,
                   lambda: lax.fori_loop(0, pos_bits, pos_step, jnp.zeros(red_shape, I32)),
                   lambda: jnp.full(red_shape, 2 ** pos_bits, I32))
    return thr, cut, any_excess


def _selected(scores, thr, cut, pos):
    return (scores > thr) | ((scores == thr) & (pos <= cut))


def _dsa_prompt_kernel(q_ref, qi_ref, wi_ref, gc_ref, k_ref, vt_ref, ki_ref, o_ref,
                       sc_ref, qt_ref, qit_ref, wit_ref, m_ref, l_ref, acc_ref, *, topk, pos_bits, ck, ca):
    qb = pl.program_id(1)
    nq = q_ref.shape[0]
    n_chunks = (qb * nq) // ck + 1
    chunk = (ck, LANES)
    stat_rows = COUNT_ACC_TILES * SUBLANES
    kpos_in = lax.broadcasted_iota(I32, chunk, 0)
    qpos = qb * nq + lax.broadcasted_iota(I32, chunk, 1)

    for h in range(AT_HEADS):
        sl = slice(h * HEAD_DIM, (h + 1) * HEAD_DIM)
        qt_ref[:, sl] = q_ref[:, sl].astype(F32).T.astype(BF16)
    for j in range(IDX_W // LANES):
        t = qi_ref[:, j * LANES:(j + 1) * LANES].astype(F32).T.astype(BF16)
        for i in range(LANES // IDX_DIM):
            h = j * (LANES // IDX_DIM) + i
            qit_ref[:, h * LANES:(h + 1) * LANES] = t[i * IDX_DIM:(i + 1) * IDX_DIM, :]
    wit_ref[...] = wi_ref[...].T[IDX_DIM:IDX_DIM + IDX_HEADS, :] * IDX_DIM ** -0.5

    def score_body(c, carry):
        off = pl.multiple_of(c * ck, ck)
        sc = _dot(ki_ref[pl.ds(off, ck), :], qit_ref[...])
        acc = jnp.zeros(chunk, F32)
        for h in range(IDX_HEADS):
            acc = acc + jnp.maximum(sc[:, h * LANES:(h + 1) * LANES], 0.0) * wit_ref[h:h + 1, :]
        ok = off + kpos_in <= qpos
        sc_ref[pl.ds(off, ck), :] = jnp.where(ok, acc, NO_SCORE)
        top, bottom, valid = carry

        def fold(x, op):
            parts = [x[r:r + stat_rows] for r in range(0, ck, stat_rows)]
            while len(parts) > 1:
                parts = [op(a, b) for a, b in zip(parts[::2], parts[1::2])]
            return parts[0]

        return (jnp.maximum(top, fold(jnp.where(ok, acc, NO_SCORE), jnp.maximum)),
                jnp.minimum(bottom, fold(jnp.where(ok, acc, -NO_SCORE), jnp.minimum)),
                valid + fold(jnp.where(ok, 1.0, 0.0), jnp.add))

    stats = lax.fori_loop(0, n_chunks, score_body,
                          (jnp.full((stat_rows, LANES), NO_SCORE, F32), jnp.full((stat_rows, LANES), -NO_SCORE, F32),
                           jnp.zeros((stat_rows, LANES), F32)))

    n_att = n_chunks * ck // ca

    def idx(c):
        return (pl.ds(pl.multiple_of(c * ca, ca), ca), slice(None))

    thr, cut, any_excess = _topk_threshold(sc_ref, idx, n_att, (ca, LANES), 0, float(topk), pos_bits, stats)

    @pl.when(any_excess)
    def _():
        def demote_body(c, carry):
            off = pl.multiple_of(c * ck, ck)
            sc = sc_ref[pl.ds(off, ck), :]
            sc_ref[pl.ds(off, ck), :] = jnp.where((sc == thr) & (off + kpos_in > cut), NO_SCORE, sc)
            return carry

        lax.fori_loop(0, n_chunks, demote_body, 0)

    thr_eff = jnp.broadcast_to(thr, (ca, LANES))

    m_ref[...] = jnp.full(m_ref.shape, NEG, F32)
    l_ref[...] = jnp.zeros(l_ref.shape, F32)
    acc_ref[...] = jnp.zeros(acc_ref.shape, F32)
    pair = 2 * HEAD_DIM
    n_pairs = AT_HEADS // 2

    def attn_body(c, carry):
        off = pl.multiple_of(c * ca, ca)
        bias = jnp.where(sc_ref[pl.ds(off, ca), :] >= thr_eff, 0.0, NEG)
        bias2 = jnp.concatenate([bias, bias], axis=1)
        scores = [_dot(k_ref[pl.ds(off, ca), ((2 * hp) // AT_GROUP) * HEAD_DIM:((2 * hp) // AT_GROUP + 1) * HEAD_DIM],
                       qt_ref[:, hp * pair:(hp + 1) * pair]) + bias2 for hp in range(n_pairs)]
        for hp, s in enumerate(scores):
            n = (2 * hp) // AT_GROUP
            sl = slice(hp * pair, (hp + 1) * pair)
            p_parts, alpha_parts = [], []
            for i in range(2):
                sh = s[:, i * HEAD_DIM:(i + 1) * HEAD_DIM]
                slh = slice(hp * pair + i * HEAD_DIM, hp * pair + (i + 1) * HEAD_DIM)
                m_old = m_ref[:, slh]
                m_new = jnp.maximum(m_old, jnp.max(sh, axis=0, keepdims=True))
                p_parts.append(jnp.exp2(sh - m_new).astype(BF16))
                alpha_parts.append(jnp.exp2(m_old - m_new))
                m_ref[:, slh] = m_new
            p = jnp.concatenate(p_parts, axis=1)
            alpha = jnp.concatenate(alpha_parts, axis=1)
            pv = _dot(vt_ref[n, :, pl.ds(off, ca)], p)
            acc_ref[:, sl] = alpha * acc_ref[:, sl] + pv[:HEAD_DIM]
            l_ref[:, sl] = alpha * l_ref[:, sl] + pv[HEAD_DIM:HEAD_DIM + 1]
        return carry

    lax.fori_loop(0, n_att, attn_body, 0)
    for h in range(AT_HEADS):
        sl = slice(h * HEAD_DIM, (h + 1) * HEAD_DIM)
        o_ref[:, sl] = (acc_ref[:, sl] / l_ref[:, sl]).T * _silu(gc_ref[:, sl])


def _dsa_prompt(q_bf, qi_bf, wi, h3, k_bf, vt_bf, ki_bf, topk):
    b, s, _ = q_bf.shape
    nq = LANES
    ca = math.gcd(s, 4 * LANES)
    ck = ca
    assert s % ca == 0
    pos_bits = max(1, int(math.ceil(math.log2(s))))
    qblk = lambda w: pl.BlockSpec((None, nq, w), lambda bi, qi: (bi, qi, 0))
    full = lambda w: pl.BlockSpec((None, s, w), lambda bi, qi: (bi, 0, 0))
    return pl.pallas_call(
        functools.partial(_dsa_prompt_kernel, topk=topk, pos_bits=pos_bits, ck=ck, ca=ca),
        grid=(b, s // nq),
        in_specs=[qblk(AT_W), qblk(IDX_W), qblk(LANES),
                  pl.BlockSpec((None, nq, AT_W), lambda bi, qi: (bi, qi, O_GC // AT_W)),
                  full(KV_W),
                  pl.BlockSpec((None, AT_KV_HEADS, VT_ROWS, s), lambda bi, qi: (bi, 0, 0, 0)),
                  full(IDX_DIM)],
        out_specs=qblk(AT_W),
        out_shape=jax.ShapeDtypeStruct((b, s, AT_W), F32),
        scratch_shapes=[pltpu.VMEM((s, LANES), F32),
                        pltpu.VMEM((HEAD_DIM, AT_HEADS * nq), BF16),
                        pltpu.VMEM((IDX_DIM, IDX_HEADS * nq), BF16),
                        pltpu.VMEM((IDX_HEADS, nq), F32),
                        pltpu.VMEM((1, AT_HEADS * nq), F32),
                        pltpu.VMEM((1, AT_HEADS * nq), F32),
                        pltpu.VMEM((HEAD_DIM, AT_HEADS * nq), F32)],
        compiler_params=_cparams(("parallel", "arbitrary")),
        name="dsa_prompt",
    )(q_bf, qi_bf, wi, h3, k_bf, vt_bf, ki_bf)


def _dsa_sample_score_kernel(pt_ref, qi_ref, wi_ref, kin_ref, *rest, n_pp, n_steps, t_pad):
    page_refs = rest[:n_pp]
    sc_ref = rest[n_pp]
    step = pl.program_id(1)
    qi = qi_ref[...]
    wi = wi_ref[...]

    def head_sum(dots):
        sc = jnp.maximum(dots, 0.0) * wi
        acc = jnp.zeros((t_pad, LANES), F32)
        for h in range(IDX_HEADS):
            acc = acc + sc[h * t_pad:(h + 1) * t_pad, :]
        return acc

    def page_scores(ki_bf):
        return head_sum(_dot_nt(qi, ki_bf))

    for j in range(n_pp):
        off = pl.multiple_of((step * n_pp + j) * PAGE_SIZE, PAGE_SIZE)
        sc_ref[:, pl.ds(off, PAGE_SIZE)] = head_sum(_dot(qi, page_refs[j][...].astype(BF16)))

    @pl.when(step == n_steps - 1)
    def _():
        past = n_steps * n_pp * PAGE_SIZE
        shape = (t_pad, LANES)
        tok = lax.broadcasted_iota(I32, shape, 0)
        lane = lax.broadcasted_iota(I32, shape, 1)
        sc_ref[:, past:past + LANES] = jnp.where(lane <= tok, page_scores(kin_ref[...]), NO_SCORE)


def _dsa_sample_select_kernel(sc_ref, selx_ref, *, topk, pos_bits):
    rows, width = sc_ref.shape
    n_chunks = width // LANES
    chunk = (rows, LANES)
    xw = AT_KV_HEADS * LANES

    def idx(c):
        return (slice(None), pl.ds(pl.multiple_of(c * LANES, LANES), LANES))

    thr, cut, _ = _topk_threshold(sc_ref, idx, n_chunks, chunk, 1, float(topk), pos_bits)

    lane = lax.broadcasted_iota(I32, chunk, 1)
    src = lax.broadcasted_iota(I32, (LANES, xw), 0)
    dst = lax.broadcasted_iota(I32, (LANES, xw), 1)
    repeat = jnp.where((dst >= src * AT_KV_HEADS) & (dst < (src + 1) * AT_KV_HEADS), 1.0, 0.0).astype(BF16)

    def sel_body(c, carry):
        sel = _selected(sc_ref[idx(c)], thr, cut, c * LANES + lane)
        sel01 = jnp.where(sel, 1.0, 0.0).astype(BF16)
        selx_ref[:, pl.ds(pl.multiple_of(c * xw, xw), xw)] = _dot(sel01, repeat).astype(BF16)
        return carry

    lax.fori_loop(0, n_chunks, sel_body, 0)


def _dsa_sample_attn_kernel(pt_ref, q_ref, selx_ref, selxn_ref, kn_ref, vn_ref, gc_ref, *rest, n_pp, n_steps, n_tok):
    k_refs = rest[:n_pp]
    v_refs = rest[n_pp:2 * n_pp]
    o_ref, m_ref, l_ref, acc_ref = rest[2 * n_pp:2 * n_pp + 4]
    step = pl.program_id(1)
    rows = n_tok * AT_HEADS
    pw = AT_KV_HEADS * PAGE_SIZE
    shape = (rows, LANES)
    row = lax.broadcasted_iota(I32, (rows, pw), 0)
    col = lax.broadcasted_iota(I32, (rows, pw), 1)
    own_head = (lax.shift_right_logical(row, GROUP_SHIFT) & (AT_KV_HEADS - 1)) == (col & (AT_KV_HEADS - 1))
    q = q_ref[...]

    @pl.when(step == 0)
    def _():
        m_ref[...] = jnp.full(m_ref.shape, NEG, F32)
        l_ref[...] = jnp.zeros(l_ref.shape, F32)
        acc_ref[...] = jnp.zeros(acc_ref.shape, F32)

    def attend(blocks):
        parts = []
        for sx, k_ref, _ in blocks:
            sel = jnp.concatenate([jnp.broadcast_to(sx[t:t + 1], (AT_HEADS, pw)) for t in range(n_tok)], axis=0)
            parts.append(_dot_nt(q, k_ref[...].astype(BF16)) + jnp.where((sel > 0.5) & own_head, 0.0, NEG))
        s = jnp.concatenate(parts, axis=1)
        m_old = m_ref[...]
        m_new = jnp.maximum(m_old, jnp.broadcast_to(jnp.max(s, axis=1, keepdims=True), shape))
        p = jnp.exp2(s - m_new[:, 0:1])
        alpha = jnp.exp2(m_old - m_new)
        l_ref[...] = alpha * l_ref[...] + jnp.broadcast_to(jnp.sum(p, axis=1, keepdims=True), shape)
        acc = alpha * acc_ref[...]
        for j, (_, _, v_ref) in enumerate(blocks):
            acc = acc + _dot(p[:, j * pw:(j + 1) * pw].astype(BF16), v_ref[...].astype(BF16))
        acc_ref[...] = acc
        m_ref[...] = m_new

    selx = selx_ref[...].astype(F32)
    attend([(selx[:, j * pw:(j + 1) * pw], k_refs[j], v_refs[j]) for j in range(n_pp)])

    @pl.when(step == n_steps - 1)
    def _():
        attend([(selxn_ref[...].astype(F32), kn_ref, vn_ref)])
        o_ref[...] = acc_ref[...] / l_ref[...] * _silu(gc_ref[...])


def _dsa_sample(q_rows, qi_st, wi_st, gc_rows, k_new, v_new, ki_new, cache_k, cache_v, cache_ik, layer, page_table,
                topk, t_pad, n_tok):
    bd = q_rows.shape[0]
    n_pages = page_table.shape[1]
    n_pp = math.gcd(n_pages, PAGES_PER_STEP)
    n_steps = n_pages // n_pp
    past = n_pages * PAGE_SIZE
    width = past + LANES
    pos_bits = max(1, int(math.ceil(math.log2(width))))
    rows_i = IDX_HEADS * t_pad
    rows_a = AT_HEADS * n_tok
    pw = AT_KV_HEADS * PAGE_SIZE
    ck = cache_k.reshape(cache_k.shape[0], cache_k.shape[1], pw, HEAD_DIM)
    cv = cache_v.reshape(cache_v.shape[0], cache_v.shape[1], pw, HEAD_DIM)

    def page_spec(rows, cols, j):
        return pl.BlockSpec((None, None, rows, cols), lambda bi, si, pt: (layer, pt[bi, si * n_pp + j], 0, 0))

    per_b3 = lambda r, w: pl.BlockSpec((None, r, w), lambda bi, si, pt: (bi, 0, 0))
    scores = pl.pallas_call(
        functools.partial(_dsa_sample_score_kernel, n_pp=n_pp, n_steps=n_steps, t_pad=t_pad),
        grid_spec=pltpu.PrefetchScalarGridSpec(
            num_scalar_prefetch=1,
            grid=(bd, n_steps),
            in_specs=[per_b3(rows_i, IDX_DIM), per_b3(rows_i, LANES), per_b3(PAGE_SIZE, IDX_DIM)]
                     + [page_spec(IDX_DIM, PAGE_SIZE, j) for j in range(n_pp)],
            out_specs=per_b3(t_pad, width)),
        out_shape=jax.ShapeDtypeStruct((bd, t_pad, width), F32),
        compiler_params=_cparams(("parallel", "arbitrary")),
        name="dsa_sample_score",
    )(page_table, qi_st, wi_st, ki_new, *([jnp.swapaxes(cache_ik, 2, 3)] * n_pp))

    n_q = bd * n_tok
    rb = min(LANES, -(-n_q // SUBLANES) * SUBLANES)
    n_q_pad = -(-n_q // rb) * rb
    scores_q = jnp.pad(scores[:, :n_tok].reshape(n_q, width), ((0, n_q_pad - n_q), (0, 0)),
                       constant_values=NO_SCORE)
    selx = pl.pallas_call(
        functools.partial(_dsa_sample_select_kernel, topk=topk, pos_bits=pos_bits),
        grid=(n_q_pad // rb,),
        in_specs=[pl.BlockSpec((rb, width), lambda i: (i, 0))],
        out_specs=pl.BlockSpec((rb, AT_KV_HEADS * width), lambda i: (i, 0)),
        out_shape=jax.ShapeDtypeStruct((n_q_pad, AT_KV_HEADS * width), BF16),
        compiler_params=_cparams(("parallel",)),
        name="dsa_sample_select",
    )(scores_q)
    selx = selx[:n_q].reshape(bd, n_tok, AT_KV_HEADS * width)

    return pl.pallas_call(
        functools.partial(_dsa_sample_attn_kernel, n_pp=n_pp, n_steps=n_steps, n_tok=n_tok),
        grid_spec=pltpu.PrefetchScalarGridSpec(
            num_scalar_prefetch=1,
            grid=(bd, n_steps),
            in_specs=[per_b3(rows_a, HEAD_DIM),
                      pl.BlockSpec((None, n_tok, n_pp * pw), lambda bi, si, pt: (bi, 0, si)),
                      pl.BlockSpec((None, n_tok, pw), lambda bi, si, pt: (bi, 0, n_pages)),
                      per_b3(pw, HEAD_DIM), per_b3(pw, HEAD_DIM), per_b3(rows_a, HEAD_DIM)]
                     + [page_spec(pw, HEAD_DIM, j) for j in range(n_pp)]
                     + [page_spec(pw, HEAD_DIM, j) for j in range(n_pp)],
            out_specs=per_b3(rows_a, HEAD_DIM),
            scratch_shapes=[pltpu.VMEM((rows_a, LANES), F32),
                            pltpu.VMEM((rows_a, LANES), F32),
                            pltpu.VMEM((rows_a, HEAD_DIM), F32)]),
        out_shape=jax.ShapeDtypeStruct((bd, rows_a, HEAD_DIM), F32),
        compiler_params=_cparams(("parallel", "arbitrary")),
        name="dsa_sample_attn",
    )(page_table, q_rows, selx, selx, k_new, v_new, gc_rows, *([ck] * n_pp), *([cv] * n_pp))


def _row_tile(m):
    return math.gcd(m, 256)


def _even_layer(x3, mk, mv, c0, n0, m0, prm, t_valid):
    norm_g, w_in_bf, bif, hg, lng, lnb, ws, bs, w_out_bf, mqn = prm
    b, t, d = x3.shape
    L = math.gcd(t, LANES)
    x2 = x3.reshape(b * t, d)
    tm = _row_tile(b * t)
    h3 = _in_proj(x2, norm_g, w_in_bf, tm).reshape(b, t, E_COLS)
    m0b = jnp.broadcast_to(m0[:, :, None], (b, ML_HEADS, LANES))
    bs_t = jnp.pad(bs[:, :L].T, ((0, 0), (0, LANES - GM_GROUPS)))
    y, vbn, c, n, mb = _even_mix(h3, c0, n0, m0b, bif, hg.reshape(1, ML_W), lng.reshape(1, GM_W),
                                 lnb.reshape(1, GM_W), ws[:, :L, :L], bs_t, mk, mv, mqn.reshape(1, HEAD_DIM),
                                 L, t_valid)
    x_new = _out_proj(x2, [y.reshape(b * t, -1)], [w_out_bf], tm).reshape(b, t, d)
    return x_new, (c, n, mb[:, :, 0]), vbn


def _odd_common(x3, mk, mv, prm, tabs, v_transposed):
    norm_g, w_in_bf, qn, kn, w_out_c, w_out_x, mqn = prm
    b, t, d = x3.shape
    x2 = x3.reshape(b * t, d)
    tm = _row_tile(b * t)
    h3 = _in_proj(x2, norm_g, w_in_bf, tm).reshape(b, t, O_COLS)
    tq = math.gcd(t, 256)
    pre = _odd_pre(h3, tabs, qn.reshape(1, HEAD_DIM), kn.reshape(1, HEAD_DIM), mk, mv, mqn.reshape(1, HEAD_DIM), tq,
                   v_transposed)
    return x2, tm, h3, pre


def _finish_odd(x2, tm, yc2, yx, prm, shape):
    w_out_c, w_out_x = prm[4], prm[5]
    return _out_proj(x2, [yc2, yx.reshape(x2.shape[0], XA_W)], [w_out_c, w_out_x], tm).reshape(shape)


def kernel(x_prompt, x_sample, state_mlstm_C, state_mlstm_n, state_mlstm_m, cache_attn_k, cache_attn_v,
           cache_idx_k, cache_mem_k, cache_mem_v, page_table, mem_prompt, norm_even, w_in_even, b_if_even,
           mlstm_hnorm, gmlp_ln_g, gmlp_ln_b, gmlp_w, gmlp_b, w_out_even, norm_odd, w_in_odd, attn_qn, attn_kn,
           w_out_odd, w_mem_kv, mem_qn, mem_kn):
    B, S, D = x_prompt.shape
    Bd, T, _ = x_sample.shape
    depth = w_mem_kv.shape[0]
    n_mem = mem_prompt.shape[1]
    past = page_table.shape[1] * PAGE_SIZE
    assert S % LANES == 0 and T <= SAMPLE_PAD

    we = w_in_even
    we = jnp.concatenate([we[..., :2048], we[..., 2056:], we[..., 2048:2056],
                          jnp.zeros(we.shape[:2] + (LANES - 2 * ML_HEADS,), we.dtype)], axis=-1).astype(BF16)
    wo = w_in_odd
    wo = jnp.concatenate([wo[..., 0:1024], wo[..., 1536:2560], wo[..., 1024:1536], wo[..., 2560:3072],
                          wo[..., 3144:4168], wo[..., 3080:3144], wo[..., 3072:3080],
                          jnp.zeros(wo.shape[:2] + (LANES - IDX_DIM - IDX_HEADS,), wo.dtype)], axis=-1).astype(BF16)
    w_out_even_bf = w_out_even.astype(BF16)
    w_out_odd_bf = w_out_odd.astype(BF16)
    bif_pad = jnp.pad(b_if_even, ((0, 0), (0, LANES - 2 * ML_HEADS)))[:, None, :]

    mem_k_p, mem_v_p = _mem_kv(mem_prompt.reshape(B * n_mem, D), w_mem_kv.astype(BF16), mem_kn,
                               _row_tile(B * n_mem))
    mem_k_p = mem_k_p.reshape(depth, B, n_mem, XA_W)
    mem_v_p = mem_v_p.reshape(depth, B, n_mem, XA_W)
    cmk = cache_mem_k.reshape(depth, Bd, n_mem * XA_HEADS, HEAD_DIM)
    cmv = cache_mem_v.reshape(depth, Bd, n_mem * XA_HEADS, HEAD_DIM)

    tp = SAMPLE_PAD
    xp = x_prompt
    xs = jnp.pad(x_sample, ((0, 0), (0, tp - T), (0, 0)))
    tabs_p = _rope_tables(jnp.arange(S))
    tabs_s = _rope_tables(past + jnp.arange(tp))
    zero_c = jnp.zeros((B, ML_HEADS, HEAD_DIM, HEAD_DIM), F32)
    zero_n = jnp.zeros((B, ML_HEADS, HEAD_DIM), F32)
    zero_m = jnp.zeros((B, ML_HEADS), F32)
    topk_p = min(TOPK_MAX, S // 4)
    topk_s = min(TOPK_MAX, (past + T) // 4)

    mlC_p, mln_p, mlm_p, mlC_s, mln_s, mlm_s, gv_s = [], [], [], [], [], [], []
    ak_p, av_p, ik_p, ak_s, av_s, ik_s = [], [], [], [], [], []
    for l in range(depth):
        if l % 2 == 0:
            e = l // 2
            prm = (norm_even[e], we[e], bif_pad[e], mlstm_hnorm[e], gmlp_ln_g[e], gmlp_ln_b[e], gmlp_w[e], gmlp_b[e],
                   w_out_even_bf[e], mem_qn[l])
            xp, (c, n, m), _ = _even_layer(xp, mem_k_p[l], mem_v_p[l], zero_c, zero_n, zero_m, prm, LANES)
            mlC_p.append(c); mln_p.append(n); mlm_p.append(m)
            xs, (c, n, m), vrows = _even_layer(xs, cmk[l], cmv[l], state_mlstm_C[e], state_mlstm_n[e],
                                               state_mlstm_m[e], prm, T)
            mlC_s.append(c); mln_s.append(n); mlm_s.append(m)
            gv_s.append(vrows[:, :T])
        else:
            o = l // 2
            prm = (norm_odd[o], wo[o], attn_qn[o], attn_kn[o], w_out_odd_bf[o, :AT_W], w_out_odd_bf[o, AT_W:],
                   mem_qn[l])
            x2, tm, h3, pre = _odd_common(xp, mem_k_p[l], mem_v_p[l], prm, tabs_p, True)
            q_bf, k, v, k_bf, vt_bf, qi_bf, ki, ki_bf, wi, yx = pre
            yc = _dsa_prompt(q_bf, qi_bf, wi, h3, k_bf, vt_bf, ki_bf, topk_p)
            xp = _finish_odd(x2, tm, yc.reshape(B * S, AT_W), yx, prm, xp.shape)
            ak_p.append(k.reshape(B, S, AT_KV_HEADS, HEAD_DIM))
            av_p.append(v.reshape(B, S, AT_KV_HEADS, HEAD_DIM))
            ik_p.append(ki)
            x2, tm, h3, pre = _odd_common(xs, cmk[l], cmv[l], prm, tabs_s, False)
            q_bf, k, v, k_bf, v_bf, qi_bf, ki, ki_bf, wi, yx = pre
            q_rows = q_bf[:, :T].reshape(Bd, T * AT_HEADS, HEAD_DIM)
            gc_rows = h3[:, :T, O_GC:O_GC + AT_W].reshape(Bd, T * AT_HEADS, HEAD_DIM)
            qi_st = qi_bf.reshape(Bd, tp, IDX_HEADS, IDX_DIM).transpose(0, 2, 1, 3).reshape(
                Bd, IDX_HEADS * tp, IDX_DIM)
            wi_h = wi[:, :, IDX_DIM:IDX_DIM + IDX_HEADS] * IDX_DIM ** -0.5
            wi_st = jnp.broadcast_to(wi_h.transpose(0, 2, 1).reshape(Bd, IDX_HEADS * tp, 1),
                                     (Bd, IDX_HEADS * tp, LANES))
            kv_page = lambda a: jnp.pad(a.reshape(Bd, tp * AT_KV_HEADS, HEAD_DIM),
                                        ((0, 0), (0, AT_KV_HEADS * (PAGE_SIZE - tp)), (0, 0)))
            ki_page = jnp.pad(ki_bf, ((0, 0), (0, PAGE_SIZE - tp), (0, 0)))
            y_rows = _dsa_sample(q_rows, qi_st, wi_st, gc_rows, kv_page(k), kv_page(v), ki_page,
                                 cache_attn_k, cache_attn_v, cache_idx_k, o, page_table, topk_s, tp, T)
            yc = jnp.pad(y_rows.reshape(Bd, T, AT_W), ((0, 0), (0, tp - T), (0, 0))).reshape(Bd * tp, AT_W)
            xs = _finish_odd(x2, tm, yc, yx, prm, xs.shape)
            ak_s.append(k[:, :T].reshape(Bd, T, AT_KV_HEADS, HEAD_DIM))
            av_s.append(v[:, :T].reshape(Bd, T, AT_KV_HEADS, HEAD_DIM))
            ik_s.append(ki[:, :T])
    return (xp, xs[:, :T],
            jnp.stack(mlC_p), jnp.stack(mln_p), jnp.stack(mlm_p),
            jnp.stack(mlC_s), jnp.stack(mln_s), jnp.stack(mlm_s),
            jnp.stack(gv_s),
            jnp.stack(ak_p), jnp.stack(av_p), jnp.stack(ik_p),
            jnp.stack(ak_s), jnp.stack(av_s), jnp.stack(ik_s),
            mem_k_p.reshape(depth, B, n_mem, XA_HEADS, HEAD_DIM),
            mem_v_p.reshape(depth, B, n_mem, XA_HEADS, HEAD_DIM))
```
